```python
import math
import jax, jax.numpy as jnp
from jax import lax
import numpy as np

D_MODEL = 2048
BATCH = 4
SEQ = 8192
DEPTH = 2

N_RET_HEADS = 8
RET_HEAD_DIM = 128
D_RET = N_RET_HEADS * RET_HEAD_DIM
RET_CHUNK = 128
ROPE_BASE = 10000.0
N_MOBA_HEADS = 8
MOBA_HEAD_DIM = 128
D_MOBA = N_MOBA_HEADS * MOBA_HEAD_DIM
MOBA_BLOCK = 256
MOBA_TOPK = 3
MOBA_Q_CHUNK = 16
D_ATTN_IN = 4 * D_RET + 3 * D_MOBA
D_RNN = 2816
N_RNN_BLOCKS = 16
RNN_BLOCK = D_RNN // N_RNN_BLOCKS
CONV_WIDTH = 4
LRU_C = 8.0
D_FF = 5632
LN_EPS = 1e-5
DEEPNORM_ALPHA = (2.0 * DEPTH) ** 0.25
DEEPNORM_BETA = (8.0 * DEPTH) ** -0.25
N_EVEN = (DEPTH + 1) // 2
N_ODD = DEPTH // 2

kernel_name = "hybrid_retention_moba_rglru_macaron_deepnorm"


def layer_norm(x, g, b):
    xf = x.astype(jnp.float32)
    mu = jnp.mean(xf, axis=-1, keepdims=True)
    var = jnp.mean(jnp.square(xf - mu), axis=-1, keepdims=True)
    return ((xf - mu) * lax.rsqrt(var + LN_EPS) * g + b).astype(x.dtype)


def swiglu(x, w_gate, w_up, w_down):
    return (jax.nn.silu(x @ w_gate) * (x @ w_up)) @ w_down


def to_heads(t, n_heads):
    b, s, _ = t.shape
    return t.reshape(b, s, n_heads, -1).transpose(0, 2, 1, 3)


def rotary(t, pos):
    d = t.shape[-1]
    half = d // 2
    inv_freq = ROPE_BASE ** (-jnp.arange(half, dtype=jnp.float32) / half)
    ang = pos[:, None] * inv_freq[None, :]
    cos, sin = jnp.cos(ang), jnp.sin(ang)
    t1, t2 = t[..., :half], t[..., half:]
    return jnp.concatenate([t1 * cos - t2 * sin, t2 * cos + t1 * sin], axis=-1)


def retention(q, k, v):
    B, H, S, dk = q.shape
    dv = v.shape[-1]
    C = RET_CHUNK
    N = S // C
    log_g = jnp.log1p(-jnp.exp2(-5.0 - jnp.arange(H, dtype=jnp.float32)))
    pos = jnp.arange(S, dtype=jnp.float32)
    q = rotary(q, pos)
    k = rotary(k, pos) * (dk ** -0.5)
    q = q.reshape(B, H, N, C, dk)
    k = k.reshape(B, H, N, C, dk)
    v = v.reshape(B, H, N, C, dv)
    c = jnp.arange(C, dtype=jnp.float32)
    rel = c[:, None] - c[None, :]
    intra = jnp.where(rel >= 0, jnp.exp(jnp.maximum(rel, 0.0)[None] * log_g[:, None, None]), 0.0)
    scores = jnp.einsum('bhncd,bhnsd->bhncs', q, k) * intra[None, :, None]
    out_inner = jnp.einsum('bhncs,bhnse->bhnce', scores, v)
    k_dec = jnp.exp((C - 1 - c)[None, :] * log_g[:, None])
    kv = jnp.einsum('bhnsd,bhnse,hs->nbhde', k, v, k_dec)
    chunk_dec = jnp.exp(C * log_g)[None, :, None, None]

    def step(state, kv_n):
        return state * chunk_dec + kv_n, state

    _, prev = lax.scan(step, jnp.zeros((B, H, dk, dv), jnp.float32), kv)
    q_dec = jnp.exp((c + 1)[None, :] * log_g[:, None])
    out_cross = jnp.einsum('bhncd,nbhde,hc->bhnce', q, prev, q_dec)
    return (out_inner + out_cross).reshape(B, H, S, dv)


def moba_attention(q, k, v):
    B, H, S, dh = q.shape
    BLK = MOBA_BLOCK
    S_pad = ((S + BLK - 1) // BLK) * BLK
    pad = ((0, 0), (0, 0), (0, S_pad - S), (0, 0))
    q, k, v = jnp.pad(q, pad), jnp.pad(k, pad), jnp.pad(v, pad)
    NB = S_pad // BLK
    k_sel_n = min(MOBA_TOPK, NB)
    scale = dh ** -0.5
    kb = k.reshape(B, H, NB, BLK, dh)
    vb = v.reshape(B, H, NB, BLK, dh)
    kmean = jnp.mean(kb.astype(jnp.float32), axis=3)
    gate = jnp.einsum('bhsd,bhnd->bhsn', q.astype(jnp.float32), kmean)
    q_block = jnp.arange(S_pad) // BLK
    past = jnp.arange(NB)[None, :] < q_block[:, None]
    gate = jnp.where(past, gate, -jnp.inf)
    _, idx = lax.top_k(gate, k_sel_n)
    valid = idx < q_block[:, None]
    bi = jnp.arange(B)[:, None, None, None]
    hi = jnp.arange(H)[None, :, None, None]
    QC = MOBA_Q_CHUNK
    n_chunks = S_pad // QC

    def chunk(ci):
        start = ci * QC
        qb = start // BLK
        qc = lax.dynamic_slice_in_dim(q, start, QC, axis=2)
        idc = lax.dynamic_slice_in_dim(idx, start, QC, axis=2)
        vac = lax.dynamic_slice_in_dim(valid, start, QC, axis=2)
        k_own = lax.dynamic_index_in_dim(kb, qb, axis=2, keepdims=False)
        v_own = lax.dynamic_index_in_dim(vb, qb, axis=2, keepdims=False)
        s_own = jnp.einsum('bhqd,bhkd->bhqk', qc, k_own).astype(jnp.float32) * scale
        qpos = start + jnp.arange(QC)
        kpos = qb * BLK + jnp.arange(BLK)
        s_own = jnp.where(kpos[None, :] <= qpos[:, None], s_own, -jnp.inf)
        k_g = kb[bi, hi, idc]
        v_g = vb[bi, hi, idc]
        s_sel = jnp.einsum('bhqd,bhqjkd->bhqjk', qc, k_g).astype(jnp.float32) * scale
        s_sel = jnp.where(vac[..., None], s_sel, -jnp.inf)
        logits = jnp.concatenate([s_own, s_sel.reshape(B, H, QC, k_sel_n * BLK)], axis=-1)
        p = jax.nn.softmax(logits, axis=-1).astype(v.dtype)
        p_own = p[..., :BLK]
        p_sel = p[..., BLK:].reshape(B, H, QC, k_sel_n, BLK)
        return (jnp.einsum('bhqk,bhkd->bhqd', p_own, v_own)
                + jnp.einsum('bhqjk,bhqjkd->bhqd', p_sel, v_g))

    out = lax.map(chunk, jnp.arange(n_chunks))
    out = jnp.moveaxis(out, 0, 2).reshape(B, H, S_pad, dh)
    return out[:, :, :S]


def attention_group_mixer(x, w_in, gn_g, gn_b, w_out):
    B, S, _ = x.shape
    proj = x @ w_in
    rq, rk, rv, rg, mq, mk, mv = jnp.split(
        proj, [D_RET, 2 * D_RET, 3 * D_RET, 4 * D_RET, 4 * D_RET + D_MOBA, 4 * D_RET + 2 * D_MOBA], axis=-1)
    ro = retention(to_heads(rq, N_RET_HEADS).astype(jnp.float32),
                   to_heads(rk, N_RET_HEADS).astype(jnp.float32),
                   to_heads(rv, N_RET_HEADS).astype(jnp.float32))
    mu = jnp.mean(ro, axis=-1, keepdims=True)
    var = jnp.mean(jnp.square(ro - mu), axis=-1, keepdims=True)
    ro = (ro - mu) * lax.rsqrt(var + LN_EPS)
    ro = ro.transpose(0, 2, 1, 3).reshape(B, S, D_RET) * gn_g + gn_b
    ro = (jax.nn.silu(rg.astype(jnp.float32)) * ro).astype(x.dtype)
    mo = moba_attention(to_heads(mq, N_MOBA_HEADS), to_heads(mk, N_MOBA_HEADS), to_heads(mv, N_MOBA_HEADS))
    mo = mo.transpose(0, 2, 1, 3).reshape(B, S, D_MOBA).astype(x.dtype)
    return jnp.concatenate([ro, mo], axis=-1) @ w_out


def rglru_mixer(x, w_in, conv_w, conv_b, ga_w, ga_b, gx_w, gx_b, lam, w_out):
    B, S, _ = x.shape
    proj = x @ w_in
    gate_br, rnn_br = jnp.split(proj, 2, axis=-1)
    gate = jax.nn.gelu(gate_br)
    u = lax.conv_general_dilated(rnn_br, conv_w[:, None, :].astype(rnn_br.dtype), window_strides=(1,),
                                 padding=[(CONV_WIDTH - 1, 0)],
                                 dimension_numbers=('NWC', 'WIO', 'NWC'),
                                 feature_group_count=D_RNN) + conv_b
    ub = u.reshape(B, S, N_RNN_BLOCKS, RNN_BLOCK)
    r = jax.nn.sigmoid((jnp.einsum('bsgi,gij->bsgj', ub, ga_w).reshape(B, S, D_RNN) + ga_b).astype(jnp.float32))
    i = jax.nn.sigmoid((jnp.einsum('bsgi,gij->bsgj', ub, gx_w).reshape(B, S, D_RNN) + gx_b).astype(jnp.float32))
    log_a = -LRU_C * r * jax.nn.softplus(-lam.astype(jnp.float32))
    a = jnp.exp(log_a)
    b = jnp.sqrt(-jnp.expm1(2.0 * log_a)) * (i * u.astype(jnp.float32))

    def combine(left, right):
        a1, b1 = left
        a2, b2 = right
        return a1 * a2, a2 * b1 + b2

    _, h = lax.associative_scan(combine, (a, b), axis=1)
    return (h.astype(x.dtype) * gate) @ w_out


def setup_inputs(seed: int = 0) -> dict:
    key = jax.random.key(seed)
    ks = jax.random.split(key, 20)
    f32 = jnp.float32
    nrm = lambda k, shape, scale: jax.random.normal(k, shape, f32) * scale
    x = jax.random.normal(ks[0], (BATCH, SEQ, D_MODEL), f32)
    ln_g = 1.0 + nrm(ks[1], (DEPTH, 3, D_MODEL), 0.01)
    ln_b = nrm(ks[2], (DEPTH, 3, D_MODEL), 0.01)
    ffn_w_gate = nrm(ks[3], (DEPTH, 2, D_MODEL, D_FF), D_MODEL ** -0.5)
    ffn_w_up = nrm(ks[4], (DEPTH, 2, D_MODEL, D_FF), D_MODEL ** -0.5)
    ffn_w_down = nrm(ks[5], (DEPTH, 2, D_FF, D_MODEL), DEEPNORM_BETA * D_FF ** -0.5)
    attn_w_in = nrm(ks[6], (N_EVEN, D_MODEL, D_ATTN_IN), D_MODEL ** -0.5)
    ret_gn_g = 1.0 + nrm(ks[7], (N_EVEN, D_RET), 0.01)
    ret_gn_b = nrm(ks[8], (N_EVEN, D_RET), 0.01)
    attn_w_out = nrm(ks[9], (N_EVEN, D_RET + D_MOBA, D_MODEL), DEEPNORM_BETA * (D_RET + D_MOBA) ** -0.5)
    rnn_w_in = nrm(ks[10], (N_ODD, D_MODEL, 2 * D_RNN), D_MODEL ** -0.5)
    rnn_conv_w = nrm(ks[11], (N_ODD, CONV_WIDTH, D_RNN), CONV_WIDTH ** -0.5)
    rnn_conv_b = nrm(ks[12], (N_ODD, D_RNN), 0.01)
    rnn_gate_a_w = nrm(ks[13], (N_ODD, N_RNN_BLOCKS, RNN_BLOCK, RNN_BLOCK), RNN_BLOCK ** -0.5)
    rnn_gate_a_b = nrm(ks[14], (N_ODD, D_RNN), 0.01)
    rnn_gate_x_w = nrm(ks[15], (N_ODD, N_RNN_BLOCKS, RNN_BLOCK, RNN_BLOCK), RNN_BLOCK ** -0.5)
    rnn_gate_x_b = nrm(ks[16], (N_ODD, D_RNN), 0.01)
    a_c = jax.random.uniform(ks[17], (N_ODD, D_RNN), f32, 0.9, 0.999)
    a0 = a_c ** (1.0 / LRU_C)
    rnn_lambda = jnp.log(a0) - jnp.log1p(-a0)
    rnn_w_out = nrm(ks[18], (N_ODD, D_RNN, D_MODEL), DEEPNORM_BETA * D_RNN ** -0.5)
    return {"x": x, "ln_g": ln_g, "ln_b": ln_b, "ffn_w_gate": ffn_w_gate, "ffn_w_up": ffn_w_up,
            "ffn_w_down": ffn_w_down, "attn_w_in": attn_w_in, "ret_gn_g": ret_gn_g, "ret_gn_b": ret_gn_b,
            "attn_w_out": attn_w_out, "rnn_w_in": rnn_w_in, "rnn_conv_w": rnn_conv_w, "rnn_conv_b": rnn_conv_b,
            "rnn_gate_a_w": rnn_gate_a_w, "rnn_gate_a_b": rnn_gate_a_b, "rnn_gate_x_w": rnn_gate_x_w,
            "rnn_gate_x_b": rnn_gate_x_b, "rnn_lambda": rnn_lambda, "rnn_w_out": rnn_w_out}


def reference(x, ln_g, ln_b, ffn_w_gate, ffn_w_up, ffn_w_down, attn_w_in, ret_gn_g, ret_gn_b, attn_w_out,
              rnn_w_in, rnn_conv_w, rnn_conv_b, rnn_gate_a_w, rnn_gate_a_b, rnn_gate_x_w, rnn_gate_x_b,
              rnn_lambda, rnn_w_out):
    h = x
    for layer in range(DEPTH):
        h = layer_norm(DEEPNORM_ALPHA * h + 0.5 * swiglu(h, ffn_w_gate[layer, 0], ffn_w_up[layer, 0],
                                                          ffn_w_down[layer, 0]), ln_g[layer, 0], ln_b[layer, 0])
        j = layer // 2
        if layer % 2 == 0:
            mix = attention_group_mixer(h, attn_w_in[j], ret_gn_g[j], ret_gn_b[j], attn_w_out[j])
        else:
            mix = rglru_mixer(h, rnn_w_in[j], rnn_conv_w[j], rnn_conv_b[j], rnn_gate_a_w[j], rnn_gate_a_b[j],
                              rnn_gate_x_w[j], rnn_gate_x_b[j], rnn_lambda[j], rnn_w_out[j])
        h = layer_norm(DEEPNORM_ALPHA * h + mix, ln_g[layer, 1], ln_b[layer, 1])
        h = layer_norm(DEEPNORM_ALPHA * h + 0.5 * swiglu(h, ffn_w_gate[layer, 1], ffn_w_up[layer, 1],
                                                          ffn_w_down[layer, 1]), ln_g[layer, 2], ln_b[layer, 2])
    return h
```

```python
import functools
import math

import jax
import jax.numpy as jnp
from jax import lax
from jax.experimental import pallas as pl
from jax.experimental.pallas import tpu as pltpu

F32 = jnp.float32
BF16 = jnp.bfloat16

D_MODEL = 2048
DEPTH = 2
N_RET_HEADS = 8
RET_HEAD_DIM = 128
D_RET = N_RET_HEADS * RET_HEAD_DIM
RET_CHUNK = 128
ROPE_BASE = 10000.0
N_MOBA_HEADS = 8
MOBA_HEAD_DIM = 128
D_MOBA = N_MOBA_HEADS * MOBA_HEAD_DIM
MOBA_BLOCK = 256
MOBA_TOPK = 3
D_RNN = 2816
N_RNN_BLOCKS = 16
RNN_BLOCK = D_RNN // N_RNN_BLOCKS
CONV_WIDTH = 4
LRU_C = 8.0
D_FF = 5632
LN_EPS = 1e-5
DEEPNORM_ALPHA = (2.0 * DEPTH) ** 0.25

RNN_SUPER = 2
RNN_SUPER_W = D_RNN // RNN_SUPER
RNN_BLOCKS_PER_SUPER = N_RNN_BLOCKS // RNN_SUPER

NEG_BIG = -1e30

VMEM_LIMIT = 56 * 1024 * 1024

NT_DIMS = (((1,), (1,)), ((), ()))
TN_DIMS = (((0,), (0,)), ((), ()))


def _params(semantics):
    return pltpu.CompilerParams(dimension_semantics=semantics, vmem_limit_bytes=VMEM_LIMIT)


def _layer_norm(y, g, b):
    mu = jnp.mean(y, axis=-1, keepdims=True)
    yc = y - mu
    var = jnp.mean(yc * yc, axis=-1, keepdims=True)
    return yc * lax.rsqrt(var + LN_EPS) * g + b


def _ffn_ln_kernel(x_ref, wg_ref, wu_ref, wd_ref, g_ref, b_ref, o_ref, xb_ref, acc_ref):
    j = pl.program_id(1)

    @pl.when(j == 0)
    def _():
        xb_ref[...] = x_ref[...].astype(BF16)
        acc_ref[...] = jnp.zeros_like(acc_ref)

    xb = xb_ref[...]
    gate = jnp.dot(xb, wg_ref[...], preferred_element_type=F32)
    up = jnp.dot(xb, wu_ref[...], preferred_element_type=F32)
    act = (gate * jax.nn.sigmoid(gate) * up).astype(BF16)
    acc_ref[...] += jnp.dot(act, wd_ref[...], preferred_element_type=F32)

    @pl.when(j == pl.num_programs(1) - 1)
    def _():
        y = DEEPNORM_ALPHA * x_ref[...] + 0.5 * acc_ref[...]
        o_ref[...] = _layer_norm(y, g_ref[...], b_ref[...])


def ffn_ln(x, wg, wu, wd, ln_g, ln_b, *, tm=512, tf=512):
    m, d = x.shape
    f = wg.shape[1]
    return pl.pallas_call(
        _ffn_ln_kernel,
        grid=(m // tm, f // tf),
        in_specs=[
            pl.BlockSpec((tm, d), lambda i, j: (i, 0)),
            pl.BlockSpec((d, tf), lambda i, j: (0, j)),
            pl.BlockSpec((d, tf), lambda i, j: (0, j)),
            pl.BlockSpec((tf, d), lambda i, j: (j, 0)),
            pl.BlockSpec((1, d), lambda i, j: (0, 0)),
            pl.BlockSpec((1, d), lambda i, j: (0, 0)),
        ],
        out_specs=pl.BlockSpec((tm, d), lambda i, j: (i, 0)),
        out_shape=jax.ShapeDtypeStruct((m, d), F32),
        scratch_shapes=[pltpu.VMEM((tm, d), BF16), pltpu.VMEM((tm, d), F32)],
        compiler_params=_params(("parallel", "arbitrary")),
        name="ffn_ln",
    )(x, wg, wu, wd, ln_g.reshape(1, d), ln_b.reshape(1, d))


def _proj_kernel(x_ref, w_ref, o_ref, xb_ref):
    @pl.when(pl.program_id(1) == 0)
    def _():
        xb_ref[...] = x_ref[...].astype(BF16)

    o_ref[...] = jnp.dot(xb_ref[...], w_ref[...], preferred_element_type=F32).astype(o_ref.dtype)


def _proj_rotary_kernel(x_ref, w_ref, cos_ref, sin_ref, o_ref, xb_ref):
    @pl.when(pl.program_id(1) == 0)
    def _():
        xb_ref[...] = x_ref[...].astype(BF16)

    acc = jnp.dot(xb_ref[...], w_ref[...], preferred_element_type=F32)
    cos = cos_ref[0]
    sin = sin_ref[0]
    hd = cos.shape[-1]
    for c in range(acc.shape[1] // hd):
        t = acc[:, c * hd:(c + 1) * hd]
        rot = pltpu.roll(t, hd // 2, 1)
        o_ref[:, c * hd:(c + 1) * hd] = (t * cos + rot * sin).astype(o_ref.dtype)


def proj(x, w, out_dtype, *, tm=1024, tn=512):
    m, k = x.shape
    n = w.shape[1]
    return pl.pallas_call(
        _proj_kernel,
        grid=(m // tm, n // tn),
        in_specs=[
            pl.BlockSpec((tm, k), lambda i, j: (i, 0)),
            pl.BlockSpec((k, tn), lambda i, j: (0, j)),
        ],
        out_specs=pl.BlockSpec((tm, tn), lambda i, j: (i, j)),
        out_shape=jax.ShapeDtypeStruct((m, n), out_dtype),
        scratch_shapes=[pltpu.VMEM((tm, k), BF16)],
        compiler_params=_params(("parallel", "arbitrary")),
        name="proj",
    )(x, w)


def proj_rotary(x, w, cos_tab, sin_tab, seq, *, tm=1024, tn=512):
    m, k = x.shape
    n = w.shape[1]
    hd = cos_tab.shape[-1]
    tm = min(tm, seq)
    per_tab = (n // 2) // tn
    tiles_per_seq = seq // tm
    tab_spec = pl.BlockSpec((1, tm, hd), lambda i, j: (j // per_tab, i % tiles_per_seq, 0))
    return pl.pallas_call(
        _proj_rotary_kernel,
        grid=(m // tm, n // tn),
        in_specs=[
            pl.BlockSpec((tm, k), lambda i, j: (i, 0)),
            pl.BlockSpec((k, tn), lambda i, j: (0, j)),
            tab_spec,
            tab_spec,
        ],
        out_specs=pl.BlockSpec((tm, tn), lambda i, j: (i, j)),
        out_shape=jax.ShapeDtypeStruct((m, n), BF16),
        scratch_shapes=[pltpu.VMEM((tm, k), BF16)],
        compiler_params=_params(("parallel", "arbitrary")),
        name="proj_rotary",
    )(x, w, cos_tab, sin_tab)


def _rotary_tables(seq):
    half = RET_HEAD_DIM // 2
    inv_freq = ROPE_BASE ** (-jnp.arange(half, dtype=F32) / half)
    ang = jnp.arange(seq, dtype=F32)[:, None] * inv_freq[None, :]
    cos, sin = jnp.cos(ang), jnp.sin(ang)
    cos2 = jnp.concatenate([cos, cos], axis=-1)
    sin2 = jnp.concatenate([-sin, sin], axis=-1)
    kscale = RET_HEAD_DIM ** -0.5
    return jnp.stack([cos2, cos2 * kscale]), jnp.stack([sin2, sin2 * kscale])


def _retention_kernel(q_ref, k_ref, v_ref, g_ref, intra_ref, kdec_ref, qdec_ref, cdec_ref,
                      gng_ref, gnb_ref, o_ref, state_ref):
    @pl.when(pl.program_id(2) == 0)
    def _():
        state_ref[...] = jnp.zeros_like(state_ref)

    intra = intra_ref[0]
    kdec = kdec_ref[0]
    qdec = qdec_ref[0]
    cdec = cdec_ref[0]
    gng = gng_ref[...]
    gnb = gnb_ref[...]
    c_len = intra.shape[0]
    state = state_ref[...]
    for c in range(q_ref.shape[0] // c_len):
        rows = slice(c * c_len, (c + 1) * c_len)
        q = q_ref[rows, :]
        k = k_ref[rows, :]
        v = v_ref[rows, :]
        scores = lax.dot_general(q, k, NT_DIMS, preferred_element_type=F32) * intra
        inner = jnp.dot(scores.astype(BF16), v, preferred_element_type=F32)
        cross = jnp.dot(q, state.astype(BF16), preferred_element_type=F32) * qdec
        out = inner + cross
        k_decayed = (k.astype(F32) * kdec).astype(BF16)
        kv = lax.dot_general(k_decayed, v, TN_DIMS, preferred_element_type=F32)
        state = state * cdec + kv
        mu = jnp.mean(out, axis=-1, keepdims=True)
        oc = out - mu
        var = jnp.mean(oc * oc, axis=-1, keepdims=True)
        normed = oc * lax.rsqrt(var + LN_EPS) * gng + gnb
        gate = g_ref[rows, :]
        o_ref[rows, :] = (gate * jax.nn.sigmoid(gate) * normed).astype(o_ref.dtype)
    state_ref[...] = state


def _retention_tables():
    h = N_RET_HEADS
    c_len = RET_CHUNK
    log_g = jnp.log1p(-jnp.exp2(-5.0 - jnp.arange(h, dtype=F32)))
    c = jnp.arange(c_len, dtype=F32)
    rel = c[:, None] - c[None, :]
    intra = jnp.where(rel >= 0, jnp.exp(jnp.maximum(rel, 0.0)[None] * log_g[:, None, None]), 0.0)
    k_dec = jnp.exp((c_len - 1 - c)[None, :] * log_g[:, None])
    q_dec = jnp.exp((c + 1)[None, :] * log_g[:, None])
    chunk_dec = jnp.exp(c_len * log_g)
    full = (h, c_len, c_len)
    return (intra.astype(F32),
            jnp.broadcast_to(k_dec[:, :, None], full),
            jnp.broadcast_to(q_dec[:, :, None], full),
            jnp.broadcast_to(chunk_dec[:, None, None], full))


def retention_gn_gate(qk, vm, rg, gn_g, gn_b, batch, seq, *, rows=1024):
    m = qk.shape[0]
    h = N_RET_HEADS
    hd = RET_HEAD_DIM
    rows = min(rows, seq)
    steps = seq // rows
    intra, kdec, qdec, cdec = _retention_tables()
    row_map = lambda off: (lambda b, hh, t: (b * steps + t, off + hh))
    tab_spec = pl.BlockSpec((1, RET_CHUNK, RET_CHUNK), lambda b, hh, t: (hh, 0, 0))
    vec_spec = pl.BlockSpec((1, hd), lambda b, hh, t: (0, hh))
    return pl.pallas_call(
        _retention_kernel,
        grid=(batch, h, steps),
        in_specs=[
            pl.BlockSpec((rows, hd), row_map(0)),
            pl.BlockSpec((rows, hd), row_map(h)),
            pl.BlockSpec((rows, hd), row_map(0)),
            pl.BlockSpec((rows, hd), row_map(0)),
            tab_spec, tab_spec, tab_spec, tab_spec,
            vec_spec, vec_spec,
        ],
        out_specs=pl.BlockSpec((rows, hd), row_map(0)),
        out_shape=jax.ShapeDtypeStruct((m, h * hd), BF16),
        scratch_shapes=[pltpu.VMEM((hd, hd), F32)],
        compiler_params=_params(("parallel", "parallel", "arbitrary")),
        name="retention",
    )(qk, qk, vm, rg, intra, kdec, qdec, cdec, gn_g.reshape(1, h * hd), gn_b.reshape(1, h * hd))


def _moba_kernel(q_ref, k_ref, v_ref, o_ref, kmean_ref, bias_ref):
    i = pl.program_id(2)
    blk = q_ref.shape[0]
    nb = k_ref.shape[0] // blk
    scale = q_ref.shape[1] ** -0.5

    @pl.when(i == 0)
    def _():
        for j in range(nb):
            kb = k_ref[j * blk:(j + 1) * blk, :].astype(F32)
            kmean_ref[j:j + 1, :] = jnp.mean(kb, axis=0, keepdims=True)

    q = q_ref[...]

    gate = lax.dot_general(kmean_ref[...].astype(BF16), q, NT_DIMS, preferred_element_type=F32)
    n_idx = lax.broadcasted_iota(jnp.int32, gate.shape, 0)
    gate = jnp.where(n_idx < i, gate, -jnp.inf)
    n_f = n_idx.astype(F32)
    bias = jnp.full(gate.shape, NEG_BIG, dtype=F32)
    for _ in range(min(MOBA_TOPK, nb)):
        best = jnp.max(gate, axis=0, keepdims=True)
        first = jnp.min(jnp.where(gate == best, n_f, float(nb)), axis=0, keepdims=True)
        hit = jnp.logical_and(n_f == first, best > -jnp.inf)
        bias = jnp.where(hit, 0.0, bias)
        gate = jnp.where(hit, -jnp.inf, gate)
    bias_ref[...] = bias

    own = pl.multiple_of(i * blk, blk)
    k_own = k_ref[pl.ds(own, blk), :]
    v_own = v_ref[pl.ds(own, blk), :]
    s = lax.dot_general(k_own, q, NT_DIMS, preferred_element_type=F32) * scale
    kpos = lax.broadcasted_iota(jnp.int32, s.shape, 0)
    qpos = lax.broadcasted_iota(jnp.int32, s.shape, 1)
    s = jnp.where(kpos <= qpos, s, NEG_BIG)
    m0 = jnp.max(s, axis=0, keepdims=True)
    p = jnp.exp(s - m0)
    l0 = jnp.sum(p, axis=0, keepdims=True)
    acc0 = lax.dot_general(v_own, p.astype(BF16), TN_DIMS, preferred_element_type=F32)

    def body(j, carry):
        m_run, l_run, acc = carry
        start = pl.multiple_of(j * blk, blk)
        k_j = k_ref[pl.ds(start, blk), :]
        v_j = v_ref[pl.ds(start, blk), :]
        s_j = lax.dot_general(k_j, q, NT_DIMS, preferred_element_type=F32) * scale
        s_j = s_j + bias_ref[pl.ds(j, 1), :]
        m_new = jnp.maximum(m_run, jnp.max(s_j, axis=0, keepdims=True))
        corr = jnp.exp(m_run - m_new)
        p_j = jnp.exp(s_j - m_new)
        l_new = l_run * corr + jnp.sum(p_j, axis=0, keepdims=True)
        acc_new = acc * corr + lax.dot_general(v_j, p_j.astype(BF16), TN_DIMS, preferred_element_type=F32)
        return m_new, l_new, acc_new

    _, l_fin, acc = lax.fori_loop(0, i, body, (m0, l0, acc0))
    o_ref[...] = (acc / l_fin).T.astype(o_ref.dtype)


def moba(vm, batch, seq):
    m = vm.shape[0]
    h = N_MOBA_HEADS
    hd = MOBA_HEAD_DIM
    blk = MOBA_BLOCK
    nb = seq // blk
    return pl.pallas_call(
        _moba_kernel,
        grid=(batch, h, nb),
        in_specs=[
            pl.BlockSpec((blk, hd), lambda b, hh, i: (b * nb + i, h + hh)),
            pl.BlockSpec((seq, hd), lambda b, hh, i: (b, 2 * h + hh)),
            pl.BlockSpec((seq, hd), lambda b, hh, i: (b, 3 * h + hh)),
        ],
        out_specs=pl.BlockSpec((blk, hd), lambda b, hh, i: (b * nb + i, hh)),
        out_shape=jax.ShapeDtypeStruct((m, h * hd), BF16),
        scratch_shapes=[pltpu.VMEM((nb, hd), F32), pltpu.VMEM((nb, blk), F32)],
        compiler_params=_params(("parallel", "parallel", "arbitrary")),
        name="moba",
    )(vm, vm, vm)


def _outproj_ln_kernel(n_in, *refs):
    h_ref = refs[0]
    a_refs = refs[1:1 + n_in]
    w_refs = refs[1 + n_in:1 + 2 * n_in]
    g_ref, b_ref, o_ref = refs[1 + 2 * n_in:]
    acc = jnp.dot(a_refs[0][...], w_refs[0][...], preferred_element_type=F32)
    for a_ref, w_ref in zip(a_refs[1:], w_refs[1:]):
        acc += jnp.dot(a_ref[...], w_ref[...], preferred_element_type=F32)
    y = DEEPNORM_ALPHA * h_ref[...] + acc
    o_ref[...] = _layer_norm(y, g_ref[...], b_ref[...])


def outproj_ln(h, acts, weights, ln_g, ln_b, *, tm=256):
    m, d = h.shape
    n_in = len(acts)
    in_specs = [pl.BlockSpec((tm, d), lambda i: (i, 0))]
    in_specs += [pl.BlockSpec((tm, a.shape[1]), lambda i: (i, 0)) for a in acts]
    in_specs += [pl.BlockSpec(w.shape, lambda i: (0, 0)) for w in weights]
    in_specs += [pl.BlockSpec((1, d), lambda i: (0, 0))] * 2
    return pl.pallas_call(
        functools.partial(_outproj_ln_kernel, n_in),
        grid=(m // tm,),
        in_specs=in_specs,
        out_specs=pl.BlockSpec((tm, d), lambda i: (i, 0)),
        out_shape=jax.ShapeDtypeStruct((m, d), F32),
        compiler_params=_params(("parallel",)),
        name="outproj_ln",
    )(h, *acts, *weights, ln_g.reshape(1, d), ln_b.reshape(1, d))


def _gelu_tanh(x):
    return 0.5 * x * (1.0 + jnp.tanh(math.sqrt(2.0 / math.pi) * (x + 0.044715 * (x * x * x))))


def _softplus(x):
    return jnp.maximum(x, 0.0) + jnp.log1p(jnp.exp(-jnp.abs(x)))


def _rglru_kernel(gate_ref, xr_ref, cw_ref, cb_ref, wa_ref, wx_ref, gab_ref, gxb_ref, lam_ref, o_ref,
                  xext_ref, a_ref, b_ref, hs_ref, carry_ref):
    tt = xr_ref.shape[0]
    halo = 8

    @pl.when(pl.program_id(2) == 0)
    def _():
        xext_ref[0:halo, :] = jnp.zeros((halo, xext_ref.shape[1]), F32)
        carry_ref[...] = jnp.zeros_like(carry_ref)

    xext_ref[halo:halo + tt, :] = xr_ref[...]
    cw = cw_ref[...]
    u = cb_ref[...]
    for tap in range(CONV_WIDTH):
        off = halo - (CONV_WIDTH - 1) + tap
        u = u + cw[tap:tap + 1, :] * xext_ref[off:off + tt, :]
    xext_ref[0:halo, :] = xext_ref[tt:tt + halo, :]

    ub = u.astype(BF16)
    r = jax.nn.sigmoid(jnp.dot(ub, wa_ref[0], preferred_element_type=F32) + gab_ref[...])
    gi = jax.nn.sigmoid(jnp.dot(ub, wx_ref[0], preferred_element_type=F32) + gxb_ref[...])
    log_a = -LRU_C * r * _softplus(-lam_ref[...])
    a = jnp.exp(log_a)
    a_ref[...] = a
    b_ref[...] = jnp.sqrt(1.0 - a * a) * (gi * u)

    def step(t, h):
        h = a_ref[pl.ds(t, 1), :] * h + b_ref[pl.ds(t, 1), :]
        hs_ref[pl.ds(t, 1), :] = h
        return h

    h_last = lax.fori_loop(0, tt, step, carry_ref[0:1, :], unroll=8)
    carry_ref[0:1, :] = h_last
    o_ref[...] = (hs_ref[...] * _gelu_tanh(gate_ref[...])).astype(o_ref.dtype)


def _super_block_diag(w):
    n, bs = RNN_BLOCKS_PER_SUPER, RNN_BLOCK
    w = w.reshape(RNN_SUPER, n, bs, bs)
    eye = jnp.eye(n, dtype=w.dtype)
    dense = jnp.einsum('sgij,gh->sgihj', w, eye)
    return dense.reshape(RNN_SUPER, n * bs, n * bs)


def rglru_core(proj_out, conv_w, conv_b, wa, wx, ga_b, gx_b, lam, batch, seq, *, tt=512):
    m = proj_out.shape[0]
    sw = RNN_SUPER_W
    tt = min(tt, seq)
    steps = seq // tt
    vec = lambda v: v.reshape(1, D_RNN)
    vec_spec = pl.BlockSpec((1, sw), lambda b, s, t: (0, s))
    w_spec = pl.BlockSpec((1, sw, sw), lambda b, s, t: (s, 0, 0))
    return pl.pallas_call(
        _rglru_kernel,
        grid=(batch, RNN_SUPER, steps),
        in_specs=[
            pl.BlockSpec((tt, sw), lambda b, s, t: (b * steps + t, s)),
            pl.BlockSpec((tt, sw), lambda b, s, t: (b * steps + t, RNN_SUPER + s)),
            pl.BlockSpec((CONV_WIDTH, sw), lambda b, s, t: (0, s)),
            vec_spec,
            w_spec, w_spec,
            vec_spec, vec_spec, vec_spec,
        ],
        out_specs=pl.BlockSpec((tt, sw), lambda b, s, t: (b * steps + t, s)),
        out_shape=jax.ShapeDtypeStruct((m, D_RNN), BF16),
        scratch_shapes=[
            pltpu.VMEM((tt + 8, sw), F32),
            pltpu.VMEM((tt, sw), F32),
            pltpu.VMEM((tt, sw), F32),
            pltpu.VMEM((tt, sw), F32),
            pltpu.VMEM((8, sw), F32),
        ],
        compiler_params=_params(("parallel", "parallel", "arbitrary")),
        name="rglru_core",
    )(proj_out, proj_out, conv_w, vec(conv_b), wa, wx, vec(ga_b), vec(gx_b), vec(lam))


def attention_mixer(h, w_in, gn_g, gn_b, w_out, ln_g, ln_b, batch, seq):
    wb = w_in.astype(BF16)
    cos_tab, sin_tab = _rotary_tables(seq)
    qk = proj_rotary(h, wb[:, :2 * D_RET], cos_tab, sin_tab, seq)
    rg = proj(h, wb[:, 3 * D_RET:4 * D_RET], F32)
    vm = proj(h, jnp.concatenate([wb[:, 2 * D_RET:3 * D_RET], wb[:, 4 * D_RET:]], axis=1), BF16)
    ro = retention_gn_gate(qk, vm, rg, gn_g, gn_b, batch, seq)
    mo = moba(vm, batch, seq)
    wo = w_out.astype(BF16)
    return outproj_ln(h, [ro, mo], [wo[:D_RET], wo[D_RET:]], ln_g, ln_b)


def rglru_mixer(h, w_in, conv_w, conv_b, ga_w, ga_b, gx_w, gx_b, lam, w_out, ln_g, ln_b, batch, seq):
    pr = proj(h, w_in.astype(BF16), F32)
    y = rglru_core(pr, conv_w, conv_b, _super_block_diag(ga_w).astype(BF16), _super_block_diag(gx_w).astype(BF16),
                   ga_b, gx_b, lam, batch, seq)
    return outproj_ln(h, [y], [w_out.astype(BF16)], ln_g, ln_b)


def kernel(x, ln_g, ln_b, ffn_w_gate, ffn_w_up, ffn_w_down, attn_w_in, ret_gn_g, ret_gn_b, attn_w_out, rnn_w_in, rnn_conv_w, rnn_conv_b, rnn_gate_a_w, rnn_gate_a_b, rnn_gate_x_w, rnn_gate_x_b, rnn_lambda, rnn_w_out):
    batch, seq, d = x.shape
    h = x.reshape(batch * seq, d)
    for layer in range(DEPTH):
        ffn = lambda hh, half, ln_i: ffn_ln(
            hh, ffn_w_gate[layer, half].astype(BF16), ffn_w_up[layer, half].astype(BF16),
            ffn_w_down[layer, half].astype(BF16), ln_g[layer, ln_i], ln_b[layer, ln_i])
        h = ffn(h, 0, 0)
        j = layer // 2
        if layer % 2 == 0:
            h = attention_mixer(h, attn_w_in[j], ret_gn_g[j], ret_gn_b[j], attn_w_out[j],
                                ln_g[layer, 1], ln_b[layer, 1], batch, seq)
        else:
            h = rglru_mixer(h, rnn_w_in[j], rnn_conv_w[j], rnn_conv_b[j], rnn_gate_a_w[j], rnn_gate_a_b[j],
                            rnn_gate_x_w[j], rnn_gate_x_b[j], rnn_lambda[j], rnn_w_out[j],
                            ln_g[layer, 1], ln_b[layer, 1], batch, seq)
        h = ffn(h, 1, 2)
    return h.reshape(batch, seq, d)
```

```python
import functools
import math

import jax
import jax.numpy as jnp
from jax import lax
from jax.experimental import pallas as pl
from jax.experimental.pallas import tpu as pltpu

F32 = jnp.float32
BF16 = jnp.bfloat16

D_MODEL = 2048
DEPTH = 2
N_RET_HEADS = 8
RET_HEAD_DIM = 128
D_RET = N_RET_HEADS * RET_HEAD_DIM
RET_CHUNK = 128
ROPE_BASE = 10000.0
N_MOBA_HEADS = 8
MOBA_HEAD_DIM = 128
D_MOBA = N_MOBA_HEADS * MOBA_HEAD_DIM
MOBA_BLOCK = 256
MOBA_TOPK = 3
D_RNN = 2816
N_RNN_BLOCKS = 16
RNN_BLOCK = D_RNN // N_RNN_BLOCKS
CONV_WIDTH = 4
LRU_C = 8.0
D_FF = 5632
LN_EPS = 1e-5
DEEPNORM_ALPHA = (2.0 * DEPTH) ** 0.25

RNN_SUPER = 2
RNN_SUPER_W = D_RNN // RNN_SUPER
RNN_BLOCKS_PER_SUPER = N_RNN_BLOCKS // RNN_SUPER

NEG_BIG = -1e30

VMEM_LIMIT = 56 * 1024 * 1024

NT_DIMS = (((1,), (1,)), ((), ()))
TN_DIMS = (((0,), (0,)), ((), ()))


def _params(semantics):
    return pltpu.CompilerParams(dimension_semantics=semantics, vmem_limit_bytes=VMEM_LIMIT)


def _layer_norm(y, g, b):
    mu = jnp.mean(y, axis=-1, keepdims=True)
    yc = y - mu
    var = jnp.mean(yc * yc, axis=-1, keepdims=True)
    return yc * lax.rsqrt(var + LN_EPS) * g + b


def _ffn_ln_kernel(x_ref, xb_ref, wg_ref, wu_ref, wd_ref, g_ref, b_ref, o_ref, ob_ref, acc_ref):
    j = pl.program_id(1)

    @pl.when(j == 0)
    def _():
        acc_ref[...] = jnp.zeros_like(acc_ref)

    xb = xb_ref[...]
    gate = jnp.dot(xb, wg_ref[...], preferred_element_type=F32)
    up = jnp.dot(xb, wu_ref[...], preferred_element_type=F32)
    act = (gate * jax.nn.sigmoid(gate) * up).astype(BF16)
    acc_ref[...] += jnp.dot(act, wd_ref[...], preferred_element_type=F32)

    @pl.when(j == pl.num_programs(1) - 1)
    def _():
        y = DEEPNORM_ALPHA * x_ref[...] + 0.5 * acc_ref[...]
        out = _layer_norm(y, g_ref[...], b_ref[...])
        o_ref[...] = out
        ob_ref[...] = out.astype(BF16)


def ffn_ln(x, xb, wg, wu, wd, ln_g, ln_b, *, tm=512, tf=512):
    m, d = x.shape
    f = wg.shape[1]
    row_spec = pl.BlockSpec((tm, d), lambda i, j: (i, 0))
    return pl.pallas_call(
        _ffn_ln_kernel,
        grid=(m // tm, f // tf),
        in_specs=[
            row_spec,
            row_spec,
            pl.BlockSpec((d, tf), lambda i, j: (0, j)),
            pl.BlockSpec((d, tf), lambda i, j: (0, j)),
            pl.BlockSpec((tf, d), lambda i, j: (j, 0)),
            pl.BlockSpec((1, d), lambda i, j: (0, 0)),
            pl.BlockSpec((1, d), lambda i, j: (0, 0)),
        ],
        out_specs=[row_spec, row_spec],
        out_shape=[jax.ShapeDtypeStruct((m, d), F32), jax.ShapeDtypeStruct((m, d), BF16)],
        scratch_shapes=[pltpu.VMEM((tm, d), F32)],
        compiler_params=_params(("parallel", "arbitrary")),
        name="ffn_ln",
    )(x, xb, wg, wu, wd, ln_g.reshape(1, d), ln_b.reshape(1, d))


def _proj_kernel(scaled_tiles, scale, x_ref, w_ref, o_ref):
    acc = jnp.dot(x_ref[...], w_ref[...], preferred_element_type=F32)
    if scaled_tiles is not None:
        j = pl.program_id(1)
        lo, hi = scaled_tiles
        acc = acc * jnp.where(jnp.logical_and(j >= lo, j < hi), scale, 1.0).astype(F32)
    o_ref[...] = acc.astype(o_ref.dtype)


def _proj_rotary_kernel(x_ref, w_ref, cos_ref, sin_ref, o_ref):
    acc = jnp.dot(x_ref[...], w_ref[...], preferred_element_type=F32)
    cos = cos_ref[0]
    sin = sin_ref[0]
    hd = cos.shape[-1]
    for c in range(acc.shape[1] // hd):
        t = acc[:, c * hd:(c + 1) * hd]
        rot = pltpu.roll(t, hd // 2, 1)
        o_ref[:, c * hd:(c + 1) * hd] = (t * cos + rot * sin).astype(o_ref.dtype)


def proj(x, w, out_dtype, *, tm=1024, tn=512, scaled_cols=None, scale=1.0):
    m, k = x.shape
    n = w.shape[1]
    scaled_tiles = None if scaled_cols is None else (scaled_cols[0] // tn, scaled_cols[1] // tn)
    return pl.pallas_call(
        functools.partial(_proj_kernel, scaled_tiles, scale),
        grid=(m // tm, n // tn),
        in_specs=[
            pl.BlockSpec((tm, k), lambda i, j: (i, 0)),
            pl.BlockSpec((k, tn), lambda i, j: (0, j)),
        ],
        out_specs=pl.BlockSpec((tm, tn), lambda i, j: (i, j)),
        out_shape=jax.ShapeDtypeStruct((m, n), out_dtype),
        compiler_params=_params(("parallel", "arbitrary")),
        name="proj",
    )(x, w)


def proj_rotary(x, w, cos_tab, sin_tab, seq, *, tm=1024, tn=512):
    m, k = x.shape
    n = w.shape[1]
    hd = cos_tab.shape[-1]
    tm = min(tm, seq)
    per_tab = (n // 2) // tn
    tiles_per_seq = seq // tm
    tab_spec = pl.BlockSpec((1, tm, hd), lambda i, j: (j // per_tab, i % tiles_per_seq, 0))
    return pl.pallas_call(
        _proj_rotary_kernel,
        grid=(m // tm, n // tn),
        in_specs=[
            pl.BlockSpec((tm, k), lambda i, j: (i, 0)),
            pl.BlockSpec((k, tn), lambda i, j: (0, j)),
            tab_spec,
            tab_spec,
        ],
        out_specs=pl.BlockSpec((tm, tn), lambda i, j: (i, j)),
        out_shape=jax.ShapeDtypeStruct((m, n), BF16),
        compiler_params=_params(("parallel", "arbitrary")),
        name="proj_rotary",
    )(x, w, cos_tab, sin_tab)


def _rotary_tables(seq):
    half = RET_HEAD_DIM // 2
    inv_freq = ROPE_BASE ** (-jnp.arange(half, dtype=F32) / half)
    ang = jnp.arange(seq, dtype=F32)[:, None] * inv_freq[None, :]
    cos, sin = jnp.cos(ang), jnp.sin(ang)
    cos2 = jnp.concatenate([cos, cos], axis=-1)
    sin2 = jnp.concatenate([-sin, sin], axis=-1)
    kscale = RET_HEAD_DIM ** -0.5
    return jnp.stack([cos2, cos2 * kscale]), jnp.stack([sin2, sin2 * kscale])


def _retention_kernel(q_ref, k_ref, v_ref, g_ref, intra_ref, kdec_ref, qdec_ref, cdec_ref,
                      gng_ref, gnb_ref, o_ref, state_ref):
    @pl.when(pl.program_id(2) == 0)
    def _():
        state_ref[...] = jnp.zeros_like(state_ref)

    intra = intra_ref[0]
    kdec = kdec_ref[0]
    qdec = qdec_ref[0]
    cdec = cdec_ref[0]
    gng = gng_ref[...]
    gnb = gnb_ref[...]
    c_len = intra.shape[0]
    state = state_ref[...]
    for c in range(q_ref.shape[0] // c_len):
        rows = slice(c * c_len, (c + 1) * c_len)
        q = q_ref[rows, :]
        k = k_ref[rows, :]
        v = v_ref[rows, :]
        scores = lax.dot_general(q, k, NT_DIMS, preferred_element_type=F32) * intra
        inner = jnp.dot(scores.astype(BF16), v, preferred_element_type=F32)
        cross = jnp.dot(q, state.astype(BF16), preferred_element_type=F32) * qdec
        out = inner + cross
        k_decayed = (k.astype(F32) * kdec).astype(BF16)
        kv = lax.dot_general(k_decayed, v, TN_DIMS, preferred_element_type=F32)
        state = state * cdec + kv
        mu = jnp.mean(out, axis=-1, keepdims=True)
        oc = out - mu
        var = jnp.mean(oc * oc, axis=-1, keepdims=True)
        normed = oc * lax.rsqrt(var + LN_EPS) * gng + gnb
        gate = g_ref[rows, :]
        o_ref[rows, :] = (gate * jax.nn.sigmoid(gate) * normed).astype(o_ref.dtype)
    state_ref[...] = state


def _retention_tables():
    h = N_RET_HEADS
    c_len = RET_CHUNK
    log_g = jnp.log1p(-jnp.exp2(-5.0 - jnp.arange(h, dtype=F32)))
    c = jnp.arange(c_len, dtype=F32)
    rel = c[:, None] - c[None, :]
    intra = jnp.where(rel >= 0, jnp.exp(jnp.maximum(rel, 0.0)[None] * log_g[:, None, None]), 0.0)
    k_dec = jnp.exp((c_len - 1 - c)[None, :] * log_g[:, None])
    q_dec = jnp.exp((c + 1)[None, :] * log_g[:, None])
    chunk_dec = jnp.exp(c_len * log_g)
    full = (h, c_len, c_len)
    return (intra.astype(F32),
            jnp.broadcast_to(k_dec[:, :, None], full),
            jnp.broadcast_to(q_dec[:, :, None], full),
            jnp.broadcast_to(chunk_dec[:, None, None], full))


def retention_gn_gate(qk, vm, rg, gn_g, gn_b, batch, seq, *, rows=1024):
    m = qk.shape[0]
    h = N_RET_HEADS
    hd = RET_HEAD_DIM
    rows = min(rows, seq)
    steps = seq // rows
    intra, kdec, qdec, cdec = _retention_tables()
    row_map = lambda off: (lambda b, hh, t: (b * steps + t, off + hh))
    tab_spec = pl.BlockSpec((1, RET_CHUNK, RET_CHUNK), lambda b, hh, t: (hh, 0, 0))
    vec_spec = pl.BlockSpec((1, hd), lambda b, hh, t: (0, hh))
    return pl.pallas_call(
        _retention_kernel,
        grid=(batch, h, steps),
        in_specs=[
            pl.BlockSpec((rows, hd), row_map(0)),
            pl.BlockSpec((rows, hd), row_map(h)),
            pl.BlockSpec((rows, hd), row_map(0)),
            pl.BlockSpec((rows, hd), row_map(0)),
            tab_spec, tab_spec, tab_spec, tab_spec,
            vec_spec, vec_spec,
        ],
        out_specs=pl.BlockSpec((rows, hd), row_map(0)),
        out_shape=jax.ShapeDtypeStruct((m, h * hd), BF16),
        scratch_shapes=[pltpu.VMEM((hd, hd), F32)],
        compiler_params=_params(("parallel", "parallel", "arbitrary")),
        name="retention",
    )(qk, qk, vm, rg, intra, kdec, qdec, cdec, gn_g.reshape(1, h * hd), gn_b.reshape(1, h * hd))


def _moba_kernel(q_ref, k_ref, v_ref, o_ref, kaug_ref, vt_ref, kmean_ref):
    i = pl.program_id(2)
    heads, nb, hd = kmean_ref.shape
    blk = q_ref.shape[0]
    n_groups, group_rows = vt_ref.shape[1], vt_ref.shape[3]
    group = group_rows // blk

    @pl.when(i == 0)
    def _():
        lane = lax.broadcasted_iota(jnp.int32, (blk, hd), 1)
        for hh in range(heads):
            cols = slice(hh * hd, (hh + 1) * hd)
            for j in range(nb):
                rows = slice(j * blk, (j + 1) * blk)
                kb = k_ref[rows, cols]
                kmean_ref[hh, j:j + 1, :] = jnp.mean(kb.astype(F32), axis=0, keepdims=True)
                kaug_ref[hh, rows, 0:hd] = kb
                kaug_ref[hh, rows, hd:2 * hd] = jnp.where(lane == j, 1.0, 0.0).astype(BF16)
            for g in range(n_groups):
                vt_ref[hh, g] = v_ref[g * group_rows:(g + 1) * group_rows, cols].T

    own = pl.multiple_of(i * blk, blk)

    def start_head(hh):
        cols = slice(hh * hd, (hh + 1) * hd)
        q = q_ref[:, cols]

        gate = lax.dot_general(kmean_ref[hh].astype(BF16), q, NT_DIMS, preferred_element_type=F32)
        n_idx = lax.broadcasted_iota(jnp.int32, gate.shape, 0)
        gate = jnp.where(n_idx < i, gate, -jnp.inf)
        n_f = n_idx.astype(F32)
        bias = jnp.full(gate.shape, NEG_BIG, dtype=F32)
        for _ in range(min(MOBA_TOPK, nb)):
            best = jnp.max(gate, axis=0, keepdims=True)
            first = jnp.min(jnp.where(gate == best, n_f, float(nb)), axis=0, keepdims=True)
            hit = jnp.logical_and(n_f == first, best > -jnp.inf)
            bias = jnp.where(hit, 0.0, bias)
            gate = jnp.where(hit, -jnp.inf, gate)
        bias_rows = jnp.concatenate([bias, jnp.zeros((hd - nb, blk), F32)], axis=0)
        q_aug = jnp.concatenate([q, bias_rows.T.astype(BF16)], axis=1)

        s = lax.dot_general(k_ref[pl.ds(own, blk), cols], q, NT_DIMS, preferred_element_type=F32)
        kpos = lax.broadcasted_iota(jnp.int32, s.shape, 0)
        qpos = lax.broadcasted_iota(jnp.int32, s.shape, 1)
        s = jnp.where(kpos <= qpos, s, NEG_BIG)
        m0 = jnp.max(s, axis=0, keepdims=True)
        p = jnp.exp(s - m0)
        l0 = jnp.sum(p, axis=0, keepdims=True)
        acc0 = lax.dot_general(v_ref[pl.ds(own, blk), cols], p.astype(BF16), TN_DIMS,
                               preferred_element_type=F32)
        return q_aug, (m0, l0, acc0)

    started = [start_head(hh) for hh in range(heads)]
    q_augs = [st[0] for st in started]

    def body(g, carry):
        start = pl.multiple_of(g * group_rows, group_rows)
        out = []
        for hh in range(heads):
            m_run, l_run, acc = carry[hh]
            s_g = lax.dot_general(kaug_ref[hh, pl.ds(start, group_rows), :], q_augs[hh], NT_DIMS,
                                  preferred_element_type=F32)
            m_new = jnp.maximum(m_run, jnp.max(s_g, axis=0, keepdims=True))
            corr = jnp.exp(m_run - m_new)
            p_g = jnp.exp(s_g - m_new)
            l_new = l_run * corr + jnp.sum(p_g, axis=0, keepdims=True)
            acc_new = acc * corr + jnp.dot(vt_ref[hh, g], p_g.astype(BF16), preferred_element_type=F32)
            out.append((m_new, l_new, acc_new))
        return tuple(out)

    past_groups = (i + (group - 1)) // group
    final = lax.fori_loop(0, past_groups, body, tuple(st[1] for st in started))
    for hh in range(heads):
        _, l_fin, acc = final[hh]
        o_ref[:, hh * hd:(hh + 1) * hd] = (acc / l_fin).T.astype(o_ref.dtype)


def moba(vm, batch, seq, *, group=4, heads_per_step=1):
    m = vm.shape[0]
    h = N_MOBA_HEADS
    hd = MOBA_HEAD_DIM
    blk = MOBA_BLOCK
    nb = seq // blk
    group = min(group, nb)
    hp = heads_per_step
    hw = hp * hd
    steps_h = h // hp
    assert nb % group == 0 and nb <= hd and h % hp == 0
    return pl.pallas_call(
        _moba_kernel,
        grid=(batch, steps_h, nb),
        in_specs=[
            pl.BlockSpec((blk, hw), lambda b, hh, i: (b * nb + i, steps_h + hh)),
            pl.BlockSpec((seq, hw), lambda b, hh, i: (b, 2 * steps_h + hh)),
            pl.BlockSpec((seq, hw), lambda b, hh, i: (b, 3 * steps_h + hh)),
        ],
        out_specs=pl.BlockSpec((blk, hw), lambda b, hh, i: (b * nb + i, hh)),
        out_shape=jax.ShapeDtypeStruct((m, h * hd), BF16),
        scratch_shapes=[
            pltpu.VMEM((hp, seq, 2 * hd), BF16),
            pltpu.VMEM((hp, nb // group, hd, group * blk), BF16),
            pltpu.VMEM((hp, nb, hd), F32),
        ],
        compiler_params=_params(("parallel", "parallel", "arbitrary")),
        name="moba",
    )(vm, vm, vm)


def _outproj_ln_kernel(n_in, *refs):
    h_ref = refs[0]
    a_refs = refs[1:1 + n_in]
    w_refs = refs[1 + n_in:1 + 2 * n_in]
    g_ref, b_ref, o_ref, ob_ref = refs[1 + 2 * n_in:]
    acc = jnp.dot(a_refs[0][...], w_refs[0][...], preferred_element_type=F32)
    for a_ref, w_ref in zip(a_refs[1:], w_refs[1:]):
        acc += jnp.dot(a_ref[...], w_ref[...], preferred_element_type=F32)
    y = DEEPNORM_ALPHA * h_ref[...] + acc
    out = _layer_norm(y, g_ref[...], b_ref[...])
    o_ref[...] = out
    ob_ref[...] = out.astype(BF16)


def outproj_ln(h, acts, weights, ln_g, ln_b, *, tm=256):
    m, d = h.shape
    n_in = len(acts)
    row_spec = pl.BlockSpec((tm, d), lambda i: (i, 0))
    in_specs = [row_spec]
    in_specs += [pl.BlockSpec((tm, a.shape[1]), lambda i: (i, 0)) for a in acts]
    in_specs += [pl.BlockSpec(w.shape, lambda i: (0, 0)) for w in weights]
    in_specs += [pl.BlockSpec((1, d), lambda i: (0, 0))] * 2
    return pl.pallas_call(
        functools.partial(_outproj_ln_kernel, n_in),
        grid=(m // tm,),
        in_specs=in_specs,
        out_specs=[row_spec, row_spec],
        out_shape=[jax.ShapeDtypeStruct((m, d), F32), jax.ShapeDtypeStruct((m, d), BF16)],
        compiler_params=_params(("parallel",)),
        name="outproj_ln",
    )(h, *acts, *weights, ln_g.reshape(1, d), ln_b.reshape(1, d))


def _gelu_tanh(x):
    return 0.5 * x * (1.0 + jnp.tanh(math.sqrt(2.0 / math.pi) * (x + 0.044715 * (x * x * x))))


def _softplus(x):
    return jnp.maximum(x, 0.0) + jnp.log1p(jnp.exp(-jnp.abs(x)))


def _rglru_kernel(gate_ref, xr_ref, cw_ref, cb_ref, wa_ref, wx_ref, gab_ref, gxb_ref, lam_ref, o_ref,
                  xext_ref, a_ref, b_ref, hs_ref, carry_ref):
    tt = xr_ref.shape[0]
    halo = 8

    @pl.when(pl.program_id(2) == 0)
    def _():
        xext_ref[0:halo, :] = jnp.zeros((halo, xext_ref.shape[1]), F32)
        carry_ref[...] = jnp.zeros_like(carry_ref)

    xext_ref[halo:halo + tt, :] = xr_ref[...]
    cw = cw_ref[...]
    u = cb_ref[...]
    for tap in range(CONV_WIDTH):
        off = halo - (CONV_WIDTH - 1) + tap
        u = u + cw[tap:tap + 1, :] * xext_ref[off:off + tt, :]
    xext_ref[0:halo, :] = xext_ref[tt:tt + halo, :]

    ub = u.astype(BF16)
    r = jax.nn.sigmoid(jnp.dot(ub, wa_ref[0], preferred_element_type=F32) + gab_ref[...])
    gi = jax.nn.sigmoid(jnp.dot(ub, wx_ref[0], preferred_element_type=F32) + gxb_ref[...])
    log_a = -LRU_C * r * _softplus(-lam_ref[...])
    a = jnp.exp(log_a)
    a_ref[...] = a
    b_ref[...] = jnp.sqrt(1.0 - a * a) * (gi * u)

    def step(t, h):
        h = a_ref[pl.ds(t, 1), :] * h + b_ref[pl.ds(t, 1), :]
        hs_ref[pl.ds(t, 1), :] = h
        return h

    h_last = lax.fori_loop(0, tt, step, carry_ref[0:1, :], unroll=8)
    carry_ref[0:1, :] = h_last
    o_ref[...] = (hs_ref[...] * _gelu_tanh(gate_ref[...])).astype(o_ref.dtype)


def _super_block_diag(w):
    n, bs = RNN_BLOCKS_PER_SUPER, RNN_BLOCK
    w = w.reshape(RNN_SUPER, n, bs, bs)
    eye = jnp.eye(n, dtype=w.dtype)
    dense = jnp.einsum('sgij,gh->sgihj', w, eye)
    return dense.reshape(RNN_SUPER, n * bs, n * bs)


def rglru_core(proj_out, conv_w, conv_b, wa, wx, ga_b, gx_b, lam, batch, seq, *, tt=512):
    m = proj_out.shape[0]
    sw = RNN_SUPER_W
    tt = min(tt, seq)
    steps = seq // tt
    vec = lambda v: v.reshape(1, D_RNN)
    vec_spec = pl.BlockSpec((1, sw), lambda b, s, t: (0, s))
    w_spec = pl.BlockSpec((1, sw, sw), lambda b, s, t: (s, 0, 0))
    return pl.pallas_call(
        _rglru_kernel,
        grid=(batch, RNN_SUPER, steps),
        in_specs=[
            pl.BlockSpec((tt, sw), lambda b, s, t: (b * steps + t, s)),
            pl.BlockSpec((tt, sw), lambda b, s, t: (b * steps + t, RNN_SUPER + s)),
            pl.BlockSpec((CONV_WIDTH, sw), lambda b, s, t: (0, s)),
            vec_spec,
            w_spec, w_spec,
            vec_spec, vec_spec, vec_spec,
        ],
        out_specs=pl.BlockSpec((tt, sw), lambda b, s, t: (b * steps + t, s)),
        out_shape=jax.ShapeDtypeStruct((m, D_RNN), BF16),
        scratch_shapes=[
            pltpu.VMEM((tt + 8, sw), F32),
            pltpu.VMEM((tt, sw), F32),
            pltpu.VMEM((tt, sw), F32),
            pltpu.VMEM((tt, sw), F32),
            pltpu.VMEM((8, sw), F32),
        ],
        compiler_params=_params(("parallel", "parallel", "arbitrary")),
        name="rglru_core",
    )(proj_out, proj_out, conv_w, vec(conv_b), wa, wx, vec(ga_b), vec(gx_b), vec(lam))


def attention_mixer(h, hb, w_in, gn_g, gn_b, w_out, ln_g, ln_b, batch, seq):
    wb = w_in.astype(BF16)
    cos_tab, sin_tab = _rotary_tables(seq)
    qk = proj_rotary(hb, wb[:, :2 * D_RET], cos_tab, sin_tab, seq)
    rg = proj(hb, wb[:, 3 * D_RET:4 * D_RET], F32)
    vm = proj(hb, jnp.concatenate([wb[:, 2 * D_RET:3 * D_RET], wb[:, 4 * D_RET:]], axis=1), BF16,
              scaled_cols=(D_RET, D_RET + D_MOBA), scale=MOBA_HEAD_DIM ** -0.5)
    ro = retention_gn_gate(qk, vm, rg, gn_g, gn_b, batch, seq)
    mo = moba(vm, batch, seq)
    wo = w_out.astype(BF16)
    return outproj_ln(h, [ro, mo], [wo[:D_RET], wo[D_RET:]], ln_g, ln_b)


def rglru_mixer(h, hb, w_in, conv_w, conv_b, ga_w, ga_b, gx_w, gx_b, lam, w_out, ln_g, ln_b, batch, seq):
    pr = proj(hb, w_in.astype(BF16), F32)
    y = rglru_core(pr, conv_w, conv_b, _super_block_diag(ga_w).astype(BF16), _super_block_diag(gx_w).astype(BF16),
                   ga_b, gx_b, lam, batch, seq)
    return outproj_ln(h, [y], [w_out.astype(BF16)], ln_g, ln_b)


def kernel(x, ln_g, ln_b, ffn_w_gate, ffn_w_up, ffn_w_down, attn_w_in, ret_gn_g, ret_gn_b, attn_w_out, rnn_w_in, rnn_conv_w, rnn_conv_b, rnn_gate_a_w, rnn_gate_a_b, rnn_gate_x_w, rnn_gate_x_b, rnn_lambda, rnn_w_out):
    batch, seq, d = x.shape
    h = x.reshape(batch * seq, d)
    hb = h.astype(BF16)
    for layer in range(DEPTH):
        ffn = lambda hh, hhb, half, ln_i: ffn_ln(
            hh, hhb, ffn_w_gate[layer, half].astype(BF16), ffn_w_up[layer, half].astype(BF16),
            ffn_w_down[layer, half].astype(BF16), ln_g[layer, ln_i], ln_b[layer, ln_i])
        h, hb = ffn(h, hb, 0, 0)
        j = layer // 2
        if layer % 2 == 0:
            h, hb = attention_mixer(h, hb, attn_w_in[j], ret_gn_g[j], ret_gn_b[j], attn_w_out[j],
                                    ln_g[layer, 1], ln_b[layer, 1], batch, seq)
        else:
            h, hb = rglru_mixer(h, hb, rnn_w_in[j], rnn_conv_w[j], rnn_conv_b[j], rnn_gate_a_w[j], rnn_gate_a_b[j],
                                rnn_gate_x_w[j], rnn_gate_x_b[j], rnn_lambda[j], rnn_w_out[j],
                                ln_g[layer, 1], ln_b[layer, 1], batch, seq)
        h, hb = ffn(h, hb, 1, 2)
    return h.reshape(batch, seq, d)
```

```python
import functools
import math

import jax
import jax.numpy as jnp
from jax import lax
from jax.experimental import pallas as pl
from jax.experimental.pallas import tpu as pltpu

F32 = jnp.float32
BF16 = jnp.bfloat16

D_MODEL = 2048
DEPTH = 2
N_RET_HEADS = 8
RET_HEAD_DIM = 128
D_RET = N_RET_HEADS * RET_HEAD_DIM
RET_CHUNK = 128
ROPE_BASE = 10000.0
N_MOBA_HEADS = 8
MOBA_HEAD_DIM = 128
D_MOBA = N_MOBA_HEADS * MOBA_HEAD_DIM
MOBA_BLOCK = 256
MOBA_TOPK = 3
D_RNN = 2816
N_RNN_BLOCKS = 16
RNN_BLOCK = D_RNN // N_RNN_BLOCKS
CONV_WIDTH = 4
LRU_C = 8.0
D_FF = 5632
LN_EPS = 1e-5
DEEPNORM_ALPHA = (2.0 * DEPTH) ** 0.25

RNN_SUPER = 2
RNN_SUPER_W = D_RNN // RNN_SUPER
RNN_BLOCKS_PER_SUPER = N_RNN_BLOCKS // RNN_SUPER

NEG_BIG = -1e30

VMEM_LIMIT = 56 * 1024 * 1024

NT_DIMS = (((1,), (1,)), ((), ()))
TN_DIMS = (((0,), (0,)), ((), ()))


def _params(semantics):
    return pltpu.CompilerParams(dimension_semantics=semantics, vmem_limit_bytes=VMEM_LIMIT)


def _layer_norm(y, g, b):
    mu = jnp.mean(y, axis=-1, keepdims=True)
    yc = y - mu
    var = jnp.mean(yc * yc, axis=-1, keepdims=True)
    return yc * lax.rsqrt(var + LN_EPS) * g + b


def _ffn_ln_kernel(emit_bf16, x_ref, wg_ref, wu_ref, wd_ref, g_ref, b_ref, *refs):
    if emit_bf16:
        o_ref, ob_ref, xb_ref, acc_ref = refs
    else:
        o_ref, xb_ref, acc_ref = refs
    j = pl.program_id(1)

    @pl.when(j == 0)
    def _():
        xb_ref[...] = x_ref[...].astype(BF16)
        acc_ref[...] = jnp.zeros_like(acc_ref)

    xb = xb_ref[...]
    gate = jnp.dot(xb, wg_ref[...], preferred_element_type=F32)
    up = jnp.dot(xb, wu_ref[...], preferred_element_type=F32)
    act = (gate * jax.nn.sigmoid(gate) * up).astype(BF16)
    acc_ref[...] += jnp.dot(act, wd_ref[...], preferred_element_type=F32)

    @pl.when(j == pl.num_programs(1) - 1)
    def _():
        y = DEEPNORM_ALPHA * x_ref[...] + 0.5 * acc_ref[...]
        out = _layer_norm(y, g_ref[...], b_ref[...])
        o_ref[...] = out
        if emit_bf16:
            ob_ref[...] = out.astype(BF16)


def ffn_ln(x, wg, wu, wd, ln_g, ln_b, *, emit_bf16, tm=512, tf=512):
    m, d = x.shape
    f = wg.shape[1]
    row_spec = pl.BlockSpec((tm, d), lambda i, j: (i, 0))
    out_specs = [row_spec]
    out_shape = [jax.ShapeDtypeStruct((m, d), F32)]
    if emit_bf16:
        out_specs.append(row_spec)
        out_shape.append(jax.ShapeDtypeStruct((m, d), BF16))
    return pl.pallas_call(
        functools.partial(_ffn_ln_kernel, emit_bf16),
        grid=(m // tm, f // tf),
        in_specs=[
            row_spec,
            pl.BlockSpec((d, tf), lambda i, j: (0, j)),
            pl.BlockSpec((d, tf), lambda i, j: (0, j)),
            pl.BlockSpec((tf, d), lambda i, j: (j, 0)),
            pl.BlockSpec((1, d), lambda i, j: (0, 0)),
            pl.BlockSpec((1, d), lambda i, j: (0, 0)),
        ],
        out_specs=out_specs,
        out_shape=out_shape,
        scratch_shapes=[pltpu.VMEM((tm, d), BF16), pltpu.VMEM((tm, d), F32)],
        compiler_params=_params(("parallel", "arbitrary")),
        name="ffn_ln",
    )(x, wg, wu, wd, ln_g.reshape(1, d), ln_b.reshape(1, d))


def _proj_kernel(scaled_tiles, scale, x_ref, w_ref, o_ref):
    acc = jnp.dot(x_ref[...], w_ref[...], preferred_element_type=F32)
    if scaled_tiles is not None:
        j = pl.program_id(1)
        lo, hi = scaled_tiles
        acc = acc * jnp.where(jnp.logical_and(j >= lo, j < hi), scale, 1.0).astype(F32)
    o_ref[...] = acc.astype(o_ref.dtype)


def _proj_rotary_kernel(x_ref, w_ref, cos_ref, sin_ref, o_ref):
    acc = jnp.dot(x_ref[...], w_ref[...], preferred_element_type=F32)
    cos = cos_ref[0]
    sin = sin_ref[0]
    hd = cos.shape[-1]
    for c in range(acc.shape[1] // hd):
        t = acc[:, c * hd:(c + 1) * hd]
        rot = pltpu.roll(t, hd // 2, 1)
        o_ref[:, c * hd:(c + 1) * hd] = (t * cos + rot * sin).astype(o_ref.dtype)


def proj(x, w, out_dtype, *, tm=1024, tn=512, scaled_cols=None, scale=1.0):
    m, k = x.shape
    n = w.shape[1]
    scaled_tiles = None if scaled_cols is None else (scaled_cols[0] // tn, scaled_cols[1] // tn)
    return pl.pallas_call(
        functools.partial(_proj_kernel, scaled_tiles, scale),
        grid=(m // tm, n // tn),
        in_specs=[
            pl.BlockSpec((tm, k), lambda i, j: (i, 0)),
            pl.BlockSpec((k, tn), lambda i, j: (0, j)),
        ],
        out_specs=pl.BlockSpec((tm, tn), lambda i, j: (i, j)),
        out_shape=jax.ShapeDtypeStruct((m, n), out_dtype),
        compiler_params=_params(("parallel", "arbitrary")),
        name="proj",
    )(x, w)


def proj_rotary(x, w, cos_tab, sin_tab, seq, *, tm=1024, tn=512):
    m, k = x.shape
    n = w.shape[1]
    hd = cos_tab.shape[-1]
    tm = min(tm, seq)
    per_tab = (n // 2) // tn
    tiles_per_seq = seq // tm
    tab_spec = pl.BlockSpec((1, tm, hd), lambda i, j: (j // per_tab, i % tiles_per_seq, 0))
    return pl.pallas_call(
        _proj_rotary_kernel,
        grid=(m // tm, n // tn),
        in_specs=[
            pl.BlockSpec((tm, k), lambda i, j: (i, 0)),
            pl.BlockSpec((k, tn), lambda i, j: (0, j)),
            tab_spec,
            tab_spec,
        ],
        out_specs=pl.BlockSpec((tm, tn), lambda i, j: (i, j)),
        out_shape=jax.ShapeDtypeStruct((m, n), BF16),
        compiler_params=_params(("parallel", "arbitrary")),
        name="proj_rotary",
    )(x, w, cos_tab, sin_tab)


def _rotary_tables(seq):
    half = RET_HEAD_DIM // 2
    inv_freq = ROPE_BASE ** (-jnp.arange(half, dtype=F32) / half)
    ang = jnp.arange(seq, dtype=F32)[:, None] * inv_freq[None, :]
    cos, sin = jnp.cos(ang), jnp.sin(ang)
    cos2 = jnp.concatenate([cos, cos], axis=-1)
    sin2 = jnp.concatenate([-sin, sin], axis=-1)
    kscale = RET_HEAD_DIM ** -0.5
    return jnp.stack([cos2, cos2 * kscale]), jnp.stack([sin2, sin2 * kscale])


def _retention_kernel(q_ref, k_ref, v_ref, g_ref, intra_ref, kdec_ref, qdec_ref, cdec_ref,
                      gng_ref, gnb_ref, o_ref, state_ref):
    @pl.when(pl.program_id(2) == 0)
    def _():
        state_ref[...] = jnp.zeros_like(state_ref)

    intra = intra_ref[0]
    kdec = kdec_ref[0]
    qdec = qdec_ref[0]
    cdec = cdec_ref[0]
    gng = gng_ref[...]
    gnb = gnb_ref[...]
    c_len = intra.shape[0]
    state = state_ref[...]
    for c in range(q_ref.shape[0] // c_len):
        rows = slice(c * c_len, (c + 1) * c_len)
        q = q_ref[rows, :]
        k = k_ref[rows, :]
        v = v_ref[rows, :]
        scores = lax.dot_general(q, k, NT_DIMS, preferred_element_type=F32) * intra
        inner = jnp.dot(scores.astype(BF16), v, preferred_element_type=F32)
        cross = jnp.dot(q, state.astype(BF16), preferred_element_type=F32) * qdec
        out = inner + cross
        k_decayed = (k.astype(F32) * kdec).astype(BF16)
        kv = lax.dot_general(k_decayed, v, TN_DIMS, preferred_element_type=F32)
        state = state * cdec + kv
        mu = jnp.mean(out, axis=-1, keepdims=True)
        oc = out - mu
        var = jnp.mean(oc * oc, axis=-1, keepdims=True)
        normed = oc * lax.rsqrt(var + LN_EPS) * gng + gnb
        gate = g_ref[rows, :]
        o_ref[rows, :] = (gate * jax.nn.sigmoid(gate) * normed).astype(o_ref.dtype)
    state_ref[...] = state


def _retention_tables():
    h = N_RET_HEADS
    c_len = RET_CHUNK
    log_g = jnp.log1p(-jnp.exp2(-5.0 - jnp.arange(h, dtype=F32)))
    c = jnp.arange(c_len, dtype=F32)
    rel = c[:, None] - c[None, :]
    intra = jnp.where(rel >= 0, jnp.exp(jnp.maximum(rel, 0.0)[None] * log_g[:, None, None]), 0.0)
    k_dec = jnp.exp((c_len - 1 - c)[None, :] * log_g[:, None])
    q_dec = jnp.exp((c + 1)[None, :] * log_g[:, None])
    chunk_dec = jnp.exp(c_len * log_g)
    full = (h, c_len, c_len)
    return (intra.astype(F32),
            jnp.broadcast_to(k_dec[:, :, None], full),
            jnp.broadcast_to(q_dec[:, :, None], full),
            jnp.broadcast_to(chunk_dec[:, None, None], full))


def retention_gn_gate(qk, vm, rg, gn_g, gn_b, batch, seq, *, rows=1024):
    m = qk.shape[0]
    h = N_RET_HEADS
    hd = RET_HEAD_DIM
    rows = min(rows, seq)
    steps = seq // rows
    intra, kdec, qdec, cdec = _retention_tables()
    row_map = lambda off: (lambda b, hh, t: (b * steps + t, off + hh))
    tab_spec = pl.BlockSpec((1, RET_CHUNK, RET_CHUNK), lambda b, hh, t: (hh, 0, 0))
    vec_spec = pl.BlockSpec((1, hd), lambda b, hh, t: (0, hh))
    return pl.pallas_call(
        _retention_kernel,
        grid=(batch, h, steps),
        in_specs=[
            pl.BlockSpec((rows, hd), row_map(0)),
            pl.BlockSpec((rows, hd), row_map(h)),
            pl.BlockSpec((rows, hd), row_map(0)),
            pl.BlockSpec((rows, hd), row_map(0)),
            tab_spec, tab_spec, tab_spec, tab_spec,
            vec_spec, vec_spec,
        ],
        out_specs=pl.BlockSpec((rows, hd), row_map(0)),
        out_shape=jax.ShapeDtypeStruct((m, h * hd), BF16),
        scratch_shapes=[pltpu.VMEM((hd, hd), F32)],
        compiler_params=_params(("parallel", "parallel", "arbitrary")),
        name="retention",
    )(qk, qk, vm, rg, intra, kdec, qdec, cdec, gn_g.reshape(1, h * hd), gn_b.reshape(1, h * hd))


def _moba_kernel(blk, q_ref, k_ref, v_ref, o_ref, kaug_ref, vt_ref, kmean_ref, qaug_ref, s_ref, m_ref, l_ref,
                 acc_ref):
    i = pl.program_id(2)
    nb, hd = kmean_ref.shape
    n_tiles, _, tile = vt_ref.shape
    per_tile = tile // blk

    @pl.when(i == 0)
    def _():
        lane = lax.broadcasted_iota(jnp.int32, (blk, hd), 1)
        for j in range(nb):
            rows = slice(j * blk, (j + 1) * blk)
            kb = k_ref[rows, :]
            kmean_ref[j:j + 1, :] = jnp.mean(kb.astype(F32), axis=0, keepdims=True)
            kaug_ref[rows, 0:hd] = kb
            kaug_ref[rows, hd:2 * hd] = jnp.where(lane == j, 1.0, 0.0).astype(BF16)
        for g in range(n_tiles):
            vt_ref[g] = v_ref[g * tile:(g + 1) * tile, :].T

    q = q_ref[...]

    gate = lax.dot_general(kmean_ref[...].astype(BF16), q, NT_DIMS, preferred_element_type=F32)
    n_idx = lax.broadcasted_iota(jnp.int32, gate.shape, 0)
    q_blk = i * per_tile + lax.broadcasted_iota(jnp.int32, gate.shape, 1) // blk
    gate = jnp.where(n_idx < q_blk, gate, -jnp.inf)
    n_f = n_idx.astype(F32)
    bias = jnp.where(n_idx == q_blk, 0.0, NEG_BIG).astype(F32)
    for _ in range(min(MOBA_TOPK, nb)):
        best = jnp.max(gate, axis=0, keepdims=True)
        first = jnp.min(jnp.where(gate == best, n_f, float(nb)), axis=0, keepdims=True)
        hit = jnp.logical_and(n_f == first, best > -jnp.inf)
        bias = jnp.where(hit, 0.0, bias)
        gate = jnp.where(hit, -jnp.inf, gate)
    bias_rows = jnp.concatenate([bias, jnp.zeros((hd - nb, tile), F32)], axis=0)
    qaug_ref[:, 0:hd] = q
    qaug_ref[:, hd:2 * hd] = bias_rows.T.astype(BF16)

    def tile_scores(g):
        start = pl.multiple_of(g * tile, tile)
        return lax.dot_general(kaug_ref[pl.ds(start, tile), :], qaug_ref[...], NT_DIMS,
                               preferred_element_type=F32)

    s = tile_scores(i)
    kpos = lax.broadcasted_iota(jnp.int32, s.shape, 0)
    qpos = lax.broadcasted_iota(jnp.int32, s.shape, 1)
    s = jnp.where(kpos <= qpos, s, NEG_BIG)
    m0 = jnp.max(s, axis=0, keepdims=True)
    p = jnp.exp(s - m0)
    m_ref[...] = m0
    l_ref[...] = jnp.sum(p, axis=0, keepdims=True)
    acc_ref[...] = jnp.dot(vt_ref[i], p.astype(BF16), preferred_element_type=F32)

    def consume(g, slot):
        s_g = s_ref[slot]
        m_run = m_ref[...]
        m_new = jnp.maximum(m_run, jnp.max(s_g, axis=0, keepdims=True))
        corr = jnp.exp(m_run - m_new)
        p_g = jnp.exp(s_g - m_new)
        m_ref[...] = m_new
        l_ref[...] = l_ref[...] * corr + jnp.sum(p_g, axis=0, keepdims=True)
        acc_ref[...] = acc_ref[...] * corr + jnp.dot(vt_ref[g], p_g.astype(BF16), preferred_element_type=F32)

    @pl.when(i > 0)
    def _():
        s_ref[0] = tile_scores(0)

    def body(g, carry):
        slot = lax.rem(g, 2)
        consume(g, slot)
        s_ref[1 - slot] = tile_scores(g + 1)
        return carry

    lax.fori_loop(0, i - 1, body, 0)

    @pl.when(i > 0)
    def _():
        consume(i - 1, lax.rem(i - 1, 2))

    o_ref[...] = (acc_ref[...] / l_ref[...]).T.astype(o_ref.dtype)


def moba(vm, batch, seq, *, blocks_per_tile=4):
    m = vm.shape[0]
    h = N_MOBA_HEADS
    hd = MOBA_HEAD_DIM
    blk = MOBA_BLOCK
    nb = seq // blk
    per_tile = min(blocks_per_tile, nb)
    tile = per_tile * blk
    n_tiles = nb // per_tile
    assert nb % per_tile == 0 and nb <= hd
    return pl.pallas_call(
        functools.partial(_moba_kernel, blk),
        grid=(batch, h, n_tiles),
        in_specs=[
            pl.BlockSpec((tile, hd), lambda b, hh, i: (b * n_tiles + i, h + hh)),
            pl.BlockSpec((seq, hd), lambda b, hh, i: (b, 2 * h + hh)),
            pl.BlockSpec((seq, hd), lambda b, hh, i: (b, 3 * h + hh)),
        ],
        out_specs=pl.BlockSpec((tile, hd), lambda b, hh, i: (b * n_tiles + i, hh)),
        out_shape=jax.ShapeDtypeStruct((m, h * hd), BF16),
        scratch_shapes=[
            pltpu.VMEM((seq, 2 * hd), BF16),
            pltpu.VMEM((n_tiles, hd, tile), BF16),
            pltpu.VMEM((nb, hd), F32),
            pltpu.VMEM((tile, 2 * hd), BF16),
            pltpu.VMEM((2, tile, tile), F32),
            pltpu.VMEM((1, tile), F32),
            pltpu.VMEM((1, tile), F32),
            pltpu.VMEM((hd, tile), F32),
        ],
        compiler_params=_params(("parallel", "parallel", "arbitrary")),
        name="moba",
    )(vm, vm, vm)


def _outproj_ln_kernel(n_in, *refs):
    h_ref = refs[0]
    a_refs = refs[1:1 + n_in]
    w_refs = refs[1 + n_in:1 + 2 * n_in]
    g_ref, b_ref, o_ref = refs[1 + 2 * n_in:]
    acc = jnp.dot(a_refs[0][...], w_refs[0][...], preferred_element_type=F32)
    for a_ref, w_ref in zip(a_refs[1:], w_refs[1:]):
        acc += jnp.dot(a_ref[...], w_ref[...], preferred_element_type=F32)
    y = DEEPNORM_ALPHA * h_ref[...] + acc
    o_ref[...] = _layer_norm(y, g_ref[...], b_ref[...])


def outproj_ln(h, acts, weights, ln_g, ln_b, *, tm=256):
    m, d = h.shape
    n_in = len(acts)
    row_spec = pl.BlockSpec((tm, d), lambda i: (i, 0))
    in_specs = [row_spec]
    in_specs += [pl.BlockSpec((tm, a.shape[1]), lambda i: (i, 0)) for a in acts]
    in_specs += [pl.BlockSpec(w.shape, lambda i: (0, 0)) for w in weights]
    in_specs += [pl.BlockSpec((1, d), lambda i: (0, 0))] * 2
    return pl.pallas_call(
        functools.partial(_outproj_ln_kernel, n_in),
        grid=(m // tm,),
        in_specs=in_specs,
        out_specs=row_spec,
        out_shape=jax.ShapeDtypeStruct((m, d), F32),
        compiler_params=_params(("parallel",)),
        name="outproj_ln",
    )(h, *acts, *weights, ln_g.reshape(1, d), ln_b.reshape(1, d))


def _gelu_tanh(x):
    return 0.5 * x * (1.0 + jnp.tanh(math.sqrt(2.0 / math.pi) * (x + 0.044715 * (x * x * x))))


def _softplus(x):
    return jnp.maximum(x, 0.0) + jnp.log1p(jnp.exp(-jnp.abs(x)))


def _rglru_kernel(gate_ref, xr_ref, cw_ref, cb_ref, wa_ref, wx_ref, gab_ref, gxb_ref, lam_ref, o_ref,
                  xext_ref, a_ref, b_ref, hs_ref, carry_ref):
    tt = xr_ref.shape[0]
    halo = 8

    @pl.when(pl.program_id(2) == 0)
    def _():
        xext_ref[0:halo, :] = jnp.zeros((halo, xext_ref.shape[1]), F32)
        carry_ref[...] = jnp.zeros_like(carry_ref)

    xext_ref[halo:halo + tt, :] = xr_ref[...]
    cw = cw_ref[...]
    u = cb_ref[...]
    for tap in range(CONV_WIDTH):
        off = halo - (CONV_WIDTH - 1) + tap
        u = u + cw[tap:tap + 1, :] * xext_ref[off:off + tt, :]
    xext_ref[0:halo, :] = xext_ref[tt:tt + halo, :]

    ub = u.astype(BF16)
    r = jax.nn.sigmoid(jnp.dot(ub, wa_ref[0], preferred_element_type=F32) + gab_ref[...])
    gi = jax.nn.sigmoid(jnp.dot(ub, wx_ref[0], preferred_element_type=F32) + gxb_ref[...])
    log_a = -LRU_C * r * _softplus(-lam_ref[...])
    a = jnp.exp(log_a)
    a_ref[...] = a
    b_ref[...] = jnp.sqrt(1.0 - a * a) * (gi * u)

    def step(t, h):
        h = a_ref[pl.ds(t, 1), :] * h + b_ref[pl.ds(t, 1), :]
        hs_ref[pl.ds(t, 1), :] = h
        return h

    h_last = lax.fori_loop(0, tt, step, carry_ref[0:1, :], unroll=8)
    carry_ref[0:1, :] = h_last
    o_ref[...] = (hs_ref[...] * _gelu_tanh(gate_ref[...])).astype(o_ref.dtype)


def _super_block_diag(w):
    n, bs = RNN_BLOCKS_PER_SUPER, RNN_BLOCK
    w = w.reshape(RNN_SUPER, n, bs, bs)
    eye = jnp.eye(n, dtype=w.dtype)
    dense = jnp.einsum('sgij,gh->sgihj', w, eye)
    return dense.reshape(RNN_SUPER, n * bs, n * bs)


def rglru_core(proj_out, conv_w, conv_b, wa, wx, ga_b, gx_b, lam, batch, seq, *, tt=512):
    m = proj_out.shape[0]
    sw = RNN_SUPER_W
    tt = min(tt, seq)
    steps = seq // tt
    vec = lambda v: v.reshape(1, D_RNN)
    vec_spec = pl.BlockSpec((1, sw), lambda b, s, t: (0, s))
    w_spec = pl.BlockSpec((1, sw, sw), lambda b, s, t: (s, 0, 0))
    return pl.pallas_call(
        _rglru_kernel,
        grid=(batch, RNN_SUPER, steps),
        in_specs=[
            pl.BlockSpec((tt, sw), lambda b, s, t: (b * steps + t, s)),
            pl.BlockSpec((tt, sw), lambda b, s, t: (b * steps + t, RNN_SUPER + s)),
            pl.BlockSpec((CONV_WIDTH, sw), lambda b, s, t: (0, s)),
            vec_spec,
            w_spec, w_spec,
            vec_spec, vec_spec, vec_spec,
        ],
        out_specs=pl.BlockSpec((tt, sw), lambda b, s, t: (b * steps + t, s)),
        out_shape=jax.ShapeDtypeStruct((m, D_RNN), BF16),
        scratch_shapes=[
            pltpu.VMEM((tt + 8, sw), F32),
            pltpu.VMEM((tt, sw), F32),
            pltpu.VMEM((tt, sw), F32),
            pltpu.VMEM((tt, sw), F32),
            pltpu.VMEM((8, sw), F32),
        ],
        compiler_params=_params(("parallel", "parallel", "arbitrary")),
        name="rglru_core",
    )(proj_out, proj_out, conv_w, vec(conv_b), wa, wx, vec(ga_b), vec(gx_b), vec(lam))


def attention_mixer(h, hb, w_in, gn_g, gn_b, w_out, ln_g, ln_b, batch, seq):
    wb = w_in.astype(BF16)
    cos_tab, sin_tab = _rotary_tables(seq)
    qk = proj_rotary(hb, wb[:, :2 * D_RET], cos_tab, sin_tab, seq)
    rg = proj(hb, wb[:, 3 * D_RET:4 * D_RET], F32)
    vm = proj(hb, jnp.concatenate([wb[:, 2 * D_RET:3 * D_RET], wb[:, 4 * D_RET:]], axis=1), BF16,
              scaled_cols=(D_RET, D_RET + D_MOBA), scale=MOBA_HEAD_DIM ** -0.5)
    ro = retention_gn_gate(qk, vm, rg, gn_g, gn_b, batch, seq)
    mo = moba(vm, batch, seq)
    wo = w_out.astype(BF16)
    return outproj_ln(h, [ro, mo], [wo[:D_RET], wo[D_RET:]], ln_g, ln_b)


def rglru_mixer(h, hb, w_in, conv_w, conv_b, ga_w, ga_b, gx_w, gx_b, lam, w_out, ln_g, ln_b, batch, seq):
    pr = proj(hb, w_in.astype(BF16), F32)
    y = rglru_core(pr, conv_w, conv_b, _super_block_diag(ga_w).astype(BF16), _super_block_diag(gx_w).astype(BF16),
                   ga_b, gx_b, lam, batch, seq)
    return outproj_ln(h, [y], [w_out.astype(BF16)], ln_g, ln_b)


def kernel(x, ln_g, ln_b, ffn_w_gate, ffn_w_up, ffn_w_down, attn_w_in, ret_gn_g, ret_gn_b, attn_w_out, rnn_w_in, rnn_conv_w, rnn_conv_b, rnn_gate_a_w, rnn_gate_a_b, rnn_gate_x_w, rnn_gate_x_b, rnn_lambda, rnn_w_out):
    batch, seq, d = x.shape
    h = x.reshape(batch * seq, d)
    for layer in range(DEPTH):
        ffn = lambda hh, half, ln_i, emit_bf16: ffn_ln(
            hh, ffn_w_gate[layer, half].astype(BF16), ffn_w_up[layer, half].astype(BF16),
            ffn_w_down[layer, half].astype(BF16), ln_g[layer, ln_i], ln_b[layer, ln_i], emit_bf16=emit_bf16)
        h, hb = ffn(h, 0, 0, True)
        j = layer // 2
        if layer % 2 == 0:
            h = attention_mixer(h, hb, attn_w_in[j], ret_gn_g[j], ret_gn_b[j], attn_w_out[j],
                                ln_g[layer, 1], ln_b[layer, 1], batch, seq)
        else:
            h = rglru_mixer(h, hb, rnn_w_in[j], rnn_conv_w[j], rnn_conv_b[j], rnn_gate_a_w[j], rnn_gate_a_b[j],
                            rnn_gate_x_w[j], rnn_gate_x_b[j], rnn_lambda[j], rnn_w_out[j],
                            ln_g[layer, 1], ln_b[layer, 1], batch, seq)
        (h,) = ffn(h, 1, 2, False)
    return h.reshape(batch, seq, d)
```

```python
import functools
import math

import jax
import jax.numpy as jnp
from jax import lax
from jax.experimental import pallas as pl
from jax.experimental.pallas import tpu as pltpu

F32 = jnp.float32
BF16 = jnp.bfloat16

D_MODEL = 2048
DEPTH = 2
N_RET_HEADS = 8
RET_HEAD_DIM = 128
D_RET = N_RET_HEADS * RET_HEAD_DIM
RET_CHUNK = 128
ROPE_BASE = 10000.0
N_MOBA_HEADS = 8
MOBA_HEAD_DIM = 128
D_MOBA = N_MOBA_HEADS * MOBA_HEAD_DIM
MOBA_BLOCK = 256
MOBA_TOPK = 3
D_RNN = 2816
N_RNN_BLOCKS = 16
RNN_BLOCK = D_RNN // N_RNN_BLOCKS
CONV_WIDTH = 4
LRU_C = 8.0
D_FF = 5632
LN_EPS = 1e-5
DEEPNORM_ALPHA = (2.0 * DEPTH) ** 0.25

RNN_SUPER = 2
RNN_SUPER_W = D_RNN // RNN_SUPER
RNN_BLOCKS_PER_SUPER = N_RNN_BLOCKS // RNN_SUPER

NEG_BIG = -1e30

VMEM_LIMIT = 56 * 1024 * 1024

NT_DIMS = (((1,), (1,)), ((), ()))
TN_DIMS = (((0,), (0,)), ((), ()))


def _params(semantics):
    return pltpu.CompilerParams(dimension_semantics=semantics, vmem_limit_bytes=VMEM_LIMIT)


def _layer_norm(y, g, b):
    mu = jnp.mean(y, axis=-1, keepdims=True)
    yc = y - mu
    var = jnp.mean(yc * yc, axis=-1, keepdims=True)
    return yc * lax.rsqrt(var + LN_EPS) * g + b


FFN_EPILOGUE_ROWS = 256


def _ffn_ln_kernel(emit_bf16, x_ref, wg_ref, wu_ref, wd_ref, g_ref, b_ref, *refs):
    if emit_bf16:
        o_ref, ob_ref, xb_ref = refs
    else:
        o_ref, xb_ref = refs
    j = pl.program_id(1)

    @pl.when(j == 0)
    def _():
        xb_ref[...] = x_ref[...].astype(BF16)
        o_ref[...] = jnp.zeros_like(o_ref)

    xb = xb_ref[...]
    gate = jnp.dot(xb, wg_ref[...], preferred_element_type=F32)
    up = jnp.dot(xb, wu_ref[...], preferred_element_type=F32)
    act = (gate * jax.nn.sigmoid(gate) * up).astype(BF16)
    o_ref[...] += jnp.dot(act, wd_ref[...], preferred_element_type=F32)

    @pl.when(j == pl.num_programs(1) - 1)
    def _():
        ln_g = g_ref[...]
        ln_b = b_ref[...]
        for r in range(0, o_ref.shape[0], FFN_EPILOGUE_ROWS):
            rows = slice(r, r + FFN_EPILOGUE_ROWS)
            y = DEEPNORM_ALPHA * x_ref[rows, :] + 0.5 * o_ref[rows, :]
            out = _layer_norm(y, ln_g, ln_b)
            o_ref[rows, :] = out
            if emit_bf16:
                ob_ref[rows, :] = out.astype(BF16)


def ffn_ln(x, wg, wu, wd, ln_g, ln_b, *, emit_bf16):
    m, d = x.shape
    f = wg.shape[1]
    tm, tf = (512, 512) if emit_bf16 else (1024, 256)
    row_spec = pl.BlockSpec((tm, d), lambda i, j: (i, 0))
    out_specs = [row_spec]
    out_shape = [jax.ShapeDtypeStruct((m, d), F32)]
    if emit_bf16:
        out_specs.append(row_spec)
        out_shape.append(jax.ShapeDtypeStruct((m, d), BF16))
    return pl.pallas_call(
        functools.partial(_ffn_ln_kernel, emit_bf16),
        grid=(m // tm, f // tf),
        in_specs=[
            row_spec,
            pl.BlockSpec((d, tf), lambda i, j: (0, j)),
            pl.BlockSpec((d, tf), lambda i, j: (0, j)),
            pl.BlockSpec((tf, d), lambda i, j: (j, 0)),
            pl.BlockSpec((1, d), lambda i, j: (0, 0)),
            pl.BlockSpec((1, d), lambda i, j: (0, 0)),
        ],
        out_specs=out_specs,
        out_shape=out_shape,
        scratch_shapes=[pltpu.VMEM((tm, d), BF16)],
        compiler_params=_params(("parallel", "arbitrary")),
        name="ffn_ln",
    )(x, wg, wu, wd, ln_g.reshape(1, d), ln_b.reshape(1, d))


def _proj_kernel(scaled_tiles, scale, col_axis, x_ref, w_ref, o_ref):
    acc = jnp.dot(x_ref[...], w_ref[...], preferred_element_type=F32)
    if scaled_tiles is not None:
        j = pl.program_id(col_axis)
        lo, hi = scaled_tiles
        acc = acc * jnp.where(jnp.logical_and(j >= lo, j < hi), scale, 1.0).astype(F32)
    o_ref[...] = acc.astype(o_ref.dtype)


def _proj_rotary_kernel(x_ref, w_ref, cos_ref, sin_ref, o_ref):
    acc = jnp.dot(x_ref[...], w_ref[...], preferred_element_type=F32)
    cos = cos_ref[0]
    sin = sin_ref[0]
    hd = cos.shape[-1]
    for c in range(acc.shape[1] // hd):
        t = acc[:, c * hd:(c + 1) * hd]
        rot = pltpu.roll(t, hd // 2, 1)
        o_ref[:, c * hd:(c + 1) * hd] = (t * cos + rot * sin).astype(o_ref.dtype)


def proj(x, w, out_dtype, *, tm=1024, tn=1024, weights_outer=False, scaled_cols=None, scale=1.0):
    m, k = x.shape
    n = w.shape[1]
    scaled_tiles = None if scaled_cols is None else (scaled_cols[0] // tn, scaled_cols[1] // tn)
    if weights_outer:
        grid = (n // tn, m // tm)
        row = lambda j, i: (i, 0)
        col = lambda j, i: (0, j)
        out = lambda j, i: (i, j)
    else:
        grid = (m // tm, n // tn)
        row = lambda i, j: (i, 0)
        col = lambda i, j: (0, j)
        out = lambda i, j: (i, j)
    return pl.pallas_call(
        functools.partial(_proj_kernel, scaled_tiles, scale, 0 if weights_outer else 1),
        grid=grid,
        in_specs=[pl.BlockSpec((tm, k), row), pl.BlockSpec((k, tn), col)],
        out_specs=pl.BlockSpec((tm, tn), out),
        out_shape=jax.ShapeDtypeStruct((m, n), out_dtype),
        compiler_params=_params(("parallel", "arbitrary")),
        name="proj",
    )(x, w)


def proj_rotary(x, w, cos_tab, sin_tab, seq, *, tm=1024, tn=1024):
    m, k = x.shape
    n = w.shape[1]
    hd = cos_tab.shape[-1]
    tm = min(tm, seq)
    per_tab = (n // 2) // tn
    tiles_per_seq = seq // tm
    tab_spec = pl.BlockSpec((1, tm, hd), lambda i, j: (j // per_tab, i % tiles_per_seq, 0))
    return pl.pallas_call(
        _proj_rotary_kernel,
        grid=(m // tm, n // tn),
        in_specs=[
            pl.BlockSpec((tm, k), lambda i, j: (i, 0)),
            pl.BlockSpec((k, tn), lambda i, j: (0, j)),
            tab_spec,
            tab_spec,
        ],
        out_specs=pl.BlockSpec((tm, tn), lambda i, j: (i, j)),
        out_shape=jax.ShapeDtypeStruct((m, n), BF16),
        compiler_params=_params(("parallel", "arbitrary")),
        name="proj_rotary",
    )(x, w, cos_tab, sin_tab)


def _rotary_tables(seq):
    half = RET_HEAD_DIM // 2
    inv_freq = ROPE_BASE ** (-jnp.arange(half, dtype=F32) / half)
    ang = jnp.arange(seq, dtype=F32)[:, None] * inv_freq[None, :]
    cos, sin = jnp.cos(ang), jnp.sin(ang)
    cos2 = jnp.concatenate([cos, cos], axis=-1)
    sin2 = jnp.concatenate([-sin, sin], axis=-1)
    kscale = RET_HEAD_DIM ** -0.5
    return jnp.stack([cos2, cos2 * kscale]), jnp.stack([sin2, sin2 * kscale])


def _retention_kernel(q_ref, k_ref, v_ref, g_ref, intra_ref, kdec_ref, qdec_ref, cdec_ref,
                      gng_ref, gnb_ref, o_ref, state_ref):
    @pl.when(pl.program_id(2) == 0)
    def _():
        state_ref[...] = jnp.zeros_like(state_ref)

    intra = intra_ref[0]
    kdec = kdec_ref[0]
    qdec = qdec_ref[0]
    cdec = cdec_ref[0]
    gng = gng_ref[...]
    gnb = gnb_ref[...]
    c_len = intra.shape[0]
    state = state_ref[...]
    for c in range(q_ref.shape[0] // c_len):
        rows = slice(c * c_len, (c + 1) * c_len)
        q = q_ref[rows, :]
        k = k_ref[rows, :]
        v = v_ref[rows, :]
        scores = lax.dot_general(q, k, NT_DIMS, preferred_element_type=F32) * intra
        inner = jnp.dot(scores.astype(BF16), v, preferred_element_type=F32)
        cross = jnp.dot(q, state.astype(BF16), preferred_element_type=F32) * qdec
        out = inner + cross
        k_decayed = (k.astype(F32) * kdec).astype(BF16)
        kv = lax.dot_general(k_decayed, v, TN_DIMS, preferred_element_type=F32)
        state = state * cdec + kv
        mu = jnp.mean(out, axis=-1, keepdims=True)
        oc = out - mu
        var = jnp.mean(oc * oc, axis=-1, keepdims=True)
        normed = oc * lax.rsqrt(var + LN_EPS) * gng + gnb
        gate = g_ref[rows, :]
        o_ref[rows, :] = (gate * jax.nn.sigmoid(gate) * normed).astype(o_ref.dtype)
    state_ref[...] = state


def _retention_tables():
    h = N_RET_HEADS
    c_len = RET_CHUNK
    log_g = jnp.log1p(-jnp.exp2(-5.0 - jnp.arange(h, dtype=F32)))
    c = jnp.arange(c_len, dtype=F32)
    rel = c[:, None] - c[None, :]
    intra = jnp.where(rel >= 0, jnp.exp(jnp.maximum(rel, 0.0)[None] * log_g[:, None, None]), 0.0)
    k_dec = jnp.exp((c_len - 1 - c)[None, :] * log_g[:, None])
    q_dec = jnp.exp((c + 1)[None, :] * log_g[:, None])
    chunk_dec = jnp.exp(c_len * log_g)
    full = (h, c_len, c_len)
    return (intra.astype(F32),
            jnp.broadcast_to(k_dec[:, :, None], full),
            jnp.broadcast_to(q_dec[:, :, None], full),
            jnp.broadcast_to(chunk_dec[:, None, None], full))


def retention_gn_gate(qk, vm, rg, gn_g, gn_b, batch, seq, *, rows=1024):
    m = qk.shape[0]
    h = N_RET_HEADS
    hd = RET_HEAD_DIM
    rows = min(rows, seq)
    steps = seq // rows
    intra, kdec, qdec, cdec = _retention_tables()
    row_map = lambda off: (lambda b, hh, t: (b * steps + t, off + hh))
    tab_spec = pl.BlockSpec((1, RET_CHUNK, RET_CHUNK), lambda b, hh, t: (hh, 0, 0))
    vec_spec = pl.BlockSpec((1, hd), lambda b, hh, t: (0, hh))
    return pl.pallas_call(
        _retention_kernel,
        grid=(batch, h, steps),
        in_specs=[
            pl.BlockSpec((rows, hd), row_map(0)),
            pl.BlockSpec((rows, hd), row_map(h)),
            pl.BlockSpec((rows, hd), row_map(0)),
            pl.BlockSpec((rows, hd), row_map(0)),
            tab_spec, tab_spec, tab_spec, tab_spec,
            vec_spec, vec_spec,
        ],
        out_specs=pl.BlockSpec((rows, hd), row_map(0)),
        out_shape=jax.ShapeDtypeStruct((m, h * hd), BF16),
        scratch_shapes=[pltpu.VMEM((hd, hd), F32)],
        compiler_params=_params(("parallel", "parallel", "arbitrary")),
        name="retention",
    )(qk, qk, vm, rg, intra, kdec, qdec, cdec, gn_g.reshape(1, h * hd), gn_b.reshape(1, h * hd))


def _moba_kernel(blk, q_ref, k_ref, v_ref, o_ref, kaug_ref, vt_ref, kmean_ref, qaug_ref, s_ref, smax_ref,
                 m_ref, l_ref, acc_ref):
    i = pl.program_id(2)
    nb, hd = kmean_ref.shape
    n_tiles, _, tile = vt_ref.shape
    per_tile = tile // blk

    @pl.when(i == 0)
    def _():
        lane = lax.broadcasted_iota(jnp.int32, (blk, hd), 1)
        for j in range(nb):
            rows = slice(j * blk, (j + 1) * blk)
            kb = k_ref[rows, :]
            kmean_ref[j:j + 1, :] = jnp.mean(kb.astype(F32), axis=0, keepdims=True)
            kaug_ref[rows, 0:hd] = kb
            kaug_ref[rows, hd:2 * hd] = jnp.where(lane == j, 1.0, 0.0).astype(BF16)
        for g in range(n_tiles):
            vt_ref[g] = v_ref[g * tile:(g + 1) * tile, :].T

    q = q_ref[...]

    gate = lax.dot_general(kmean_ref[...].astype(BF16), q, NT_DIMS, preferred_element_type=F32)
    n_idx = lax.broadcasted_iota(jnp.int32, gate.shape, 0)
    q_blk = i * per_tile + lax.broadcasted_iota(jnp.int32, gate.shape, 1) // blk
    gate = jnp.where(n_idx < q_blk, gate, -jnp.inf)
    n_f = n_idx.astype(F32)
    bias = jnp.where(n_idx == q_blk, 0.0, NEG_BIG).astype(F32)
    for _ in range(min(MOBA_TOPK, nb)):
        best = jnp.max(gate, axis=0, keepdims=True)
        first = jnp.min(jnp.where(gate == best, n_f, float(nb)), axis=0, keepdims=True)
        hit = jnp.logical_and(n_f == first, best > -jnp.inf)
        bias = jnp.where(hit, 0.0, bias)
        gate = jnp.where(hit, -jnp.inf, gate)
    bias_rows = jnp.concatenate([bias, jnp.zeros((hd - nb, tile), F32)], axis=0)
    qaug_ref[:, 0:hd] = q
    qaug_ref[:, hd:2 * hd] = bias_rows.T.astype(BF16)

    def produce(slot, g):
        start = pl.multiple_of(g * tile, tile)
        s = lax.dot_general(kaug_ref[pl.ds(start, tile), :], qaug_ref[...], NT_DIMS,
                            preferred_element_type=F32)
        s_ref[slot] = s
        smax_ref[slot] = jnp.max(s, axis=0, keepdims=True)

    m_ref[...] = jnp.full(m_ref.shape, NEG_BIG, F32)
    l_ref[...] = jnp.zeros_like(l_ref)
    acc_ref[...] = jnp.zeros_like(acc_ref)

    def consume(g, s_g, s_max):
        m_run = m_ref[...]
        m_new = jnp.maximum(m_run, s_max)
        corr = jnp.exp(m_run - m_new)
        p_g = jnp.exp(s_g - m_new)
        m_ref[...] = m_new
        l_ref[...] = l_ref[...] * corr + jnp.sum(p_g, axis=0, keepdims=True)
        acc_ref[...] = acc_ref[...] * corr + jnp.dot(vt_ref[g], p_g.astype(BF16), preferred_element_type=F32)

    def consume_own(slot):
        s = s_ref[slot]
        kpos = lax.broadcasted_iota(jnp.int32, s.shape, 0)
        qpos = lax.broadcasted_iota(jnp.int32, s.shape, 1)
        s = jnp.where(kpos <= qpos, s, NEG_BIG)
        consume(i, s, jnp.max(s, axis=0, keepdims=True))

    produce(0, 0)

    def two_tiles(t, carry):
        g = 2 * t
        consume(g, s_ref[0], smax_ref[0])
        produce(1, g + 1)
        consume(g + 1, s_ref[1], smax_ref[1])
        produce(0, g + 2)
        return carry

    lax.fori_loop(0, i // 2, two_tiles, 0)

    @pl.when(i % 2 == 1)
    def _():
        consume(i - 1, s_ref[0], smax_ref[0])
        produce(1, i)
        consume_own(1)

    @pl.when(i % 2 == 0)
    def _():
        consume_own(0)

    o_ref[...] = (acc_ref[...] / l_ref[...]).T.astype(o_ref.dtype)


def moba(vm, batch, seq, *, blocks_per_tile=4):
    m = vm.shape[0]
    h = N_MOBA_HEADS
    hd = MOBA_HEAD_DIM
    blk = MOBA_BLOCK
    nb = seq // blk
    per_tile = min(blocks_per_tile, nb)
    tile = per_tile * blk
    n_tiles = nb // per_tile
    assert nb % per_tile == 0 and nb <= hd
    return pl.pallas_call(
        functools.partial(_moba_kernel, blk),
        grid=(batch, h, n_tiles),
        in_specs=[
            pl.BlockSpec((tile, hd), lambda b, hh, i: (b * n_tiles + i, h + hh)),
            pl.BlockSpec((seq, hd), lambda b, hh, i: (b, 2 * h + hh)),
            pl.BlockSpec((seq, hd), lambda b, hh, i: (b, 3 * h + hh)),
        ],
        out_specs=pl.BlockSpec((tile, hd), lambda b, hh, i: (b * n_tiles + i, hh)),
        out_shape=jax.ShapeDtypeStruct((m, h * hd), BF16),
        scratch_shapes=[
            pltpu.VMEM((seq, 2 * hd), BF16),
            pltpu.VMEM((n_tiles, hd, tile), BF16),
            pltpu.VMEM((nb, hd), F32),
            pltpu.VMEM((tile, 2 * hd), BF16),
            pltpu.VMEM((2, tile, tile), F32),
            pltpu.VMEM((2, 1, tile), F32),
            pltpu.VMEM((1, tile), F32),
            pltpu.VMEM((1, tile), F32),
            pltpu.VMEM((hd, tile), F32),
        ],
        compiler_params=_params(("parallel", "parallel", "arbitrary")),
        name="moba",
    )(vm, vm, vm)


def _outproj_ln_kernel(n_in, *refs):
    h_ref = refs[0]
    a_refs = refs[1:1 + n_in]
    w_refs = refs[1 + n_in:1 + 2 * n_in]
    g_ref, b_ref, o_ref = refs[1 + 2 * n_in:]
    acc = jnp.dot(a_refs[0][...], w_refs[0][...], preferred_element_type=F32)
    for a_ref, w_ref in zip(a_refs[1:], w_refs[1:]):
        acc += jnp.dot(a_ref[...], w_ref[...], preferred_element_type=F32)
    y = DEEPNORM_ALPHA * h_ref[...] + acc
    o_ref[...] = _layer_norm(y, g_ref[...], b_ref[...])


def outproj_ln(h, acts, weights, ln_g, ln_b, *, tm=256):
    m, d = h.shape
    n_in = len(acts)
    row_spec = pl.BlockSpec((tm, d), lambda i: (i, 0))
    in_specs = [row_spec]
    in_specs += [pl.BlockSpec((tm, a.shape[1]), lambda i: (i, 0)) for a in acts]
    in_specs += [pl.BlockSpec(w.shape, lambda i: (0, 0)) for w in weights]
    in_specs += [pl.BlockSpec((1, d), lambda i: (0, 0))] * 2
    return pl.pallas_call(
        functools.partial(_outproj_ln_kernel, n_in),
        grid=(m // tm,),
        in_specs=in_specs,
        out_specs=row_spec,
        out_shape=jax.ShapeDtypeStruct((m, d), F32),
        compiler_params=_params(("parallel",)),
        name="outproj_ln",
    )(h, *acts, *weights, ln_g.reshape(1, d), ln_b.reshape(1, d))


def _gelu_tanh(x):
    return 0.5 * x * (1.0 + jnp.tanh(math.sqrt(2.0 / math.pi) * (x + 0.044715 * (x * x * x))))


def _softplus(x):
    return jnp.maximum(x, 0.0) + jnp.log1p(jnp.exp(-jnp.abs(x)))


def _rglru_kernel(gate_ref, xr_ref, cw_ref, cb_ref, wa_ref, wx_ref, gab_ref, gxb_ref, lam_ref, o_ref,
                  xext_ref, a_ref, b_ref, hs_ref, carry_ref):
    tt = xr_ref.shape[0]
    halo = 8

    @pl.when(pl.program_id(2) == 0)
    def _():
        xext_ref[0:halo, :] = jnp.zeros((halo, xext_ref.shape[1]), F32)
        carry_ref[...] = jnp.zeros_like(carry_ref)

    xext_ref[halo:halo + tt, :] = xr_ref[...]
    cw = cw_ref[...]
    u = cb_ref[...]
    for tap in range(CONV_WIDTH):
        off = halo - (CONV_WIDTH - 1) + tap
        u = u + cw[tap:tap + 1, :] * xext_ref[off:off + tt, :]
    xext_ref[0:halo, :] = xext_ref[tt:tt + halo, :]

    ub = u.astype(BF16)
    sw = ub.shape[1]

    def gate_pre(w_ref):
        parts = [jnp.dot(ub[:, ks:ks + RNN_GATE_WINDOW], w_ref[0, n], preferred_element_type=F32)
                 for n, ks in enumerate(RNN_GATE_WINDOW_STARTS)]
        return jnp.concatenate(parts, axis=1)[:, :sw]

    r = jax.nn.sigmoid(gate_pre(wa_ref) + gab_ref[...])
    gi = jax.nn.sigmoid(gate_pre(wx_ref) + gxb_ref[...])
    log_a = -LRU_C * r * _softplus(-lam_ref[...])
    a = jnp.exp(log_a)
    a_ref[...] = a
    b_ref[...] = jnp.sqrt(1.0 - a * a) * (gi * u)

    def step(t, h):
        h = a_ref[pl.ds(t, 1), :] * h + b_ref[pl.ds(t, 1), :]
        hs_ref[pl.ds(t, 1), :] = h
        return h

    h_last = lax.fori_loop(0, tt, step, carry_ref[0:1, :], unroll=8)
    carry_ref[0:1, :] = h_last
    o_ref[...] = (hs_ref[...] * _gelu_tanh(gate_ref[...])).astype(o_ref.dtype)


def _gate_windows():
    spans = []
    for lo in range(0, RNN_SUPER_W, RNN_GATE_TILE):
        hi = min(lo + RNN_GATE_TILE, RNN_SUPER_W)
        first, last = lo // RNN_BLOCK, (hi - 1) // RNN_BLOCK
        spans.append((first * RNN_BLOCK // 128 * 128, -(-(last + 1) * RNN_BLOCK // 128) * 128))
    width = max(e - s for s, e in spans)
    return width, tuple(min(s, RNN_SUPER_W - width) for s, _ in spans)


RNN_GATE_TILE = 256
RNN_GATE_WINDOW, RNN_GATE_WINDOW_STARTS = _gate_windows()


def _windowed_block_diag(w):
    n, bs = RNN_BLOCKS_PER_SUPER, RNN_BLOCK
    w = w.reshape(RNN_SUPER, n, bs, bs)
    eye = jnp.eye(n, dtype=w.dtype)
    dense = jnp.einsum('sgij,gh->sgihj', w, eye).reshape(RNN_SUPER, n * bs, n * bs)
    n_tiles = len(RNN_GATE_WINDOW_STARTS)
    dense = jnp.pad(dense, ((0, 0), (0, 0), (0, n_tiles * RNN_GATE_TILE - RNN_SUPER_W)))
    tiles = [dense[:, ks:ks + RNN_GATE_WINDOW, t * RNN_GATE_TILE:(t + 1) * RNN_GATE_TILE]
             for t, ks in enumerate(RNN_GATE_WINDOW_STARTS)]
    return jnp.stack(tiles, axis=1)


def rglru_core(proj_out, conv_w, conv_b, wa, wx, ga_b, gx_b, lam, batch, seq, *, tt=512):
    m = proj_out.shape[0]
    sw = RNN_SUPER_W
    tt = min(tt, seq)
    steps = seq // tt
    vec = lambda v: v.reshape(1, D_RNN)
    vec_spec = pl.BlockSpec((1, sw), lambda b, s, t: (0, s))
    w_spec = pl.BlockSpec((1,) + wa.shape[1:], lambda b, s, t: (s, 0, 0, 0))
    return pl.pallas_call(
        _rglru_kernel,
        grid=(batch, RNN_SUPER, steps),
        in_specs=[
            pl.BlockSpec((tt, sw), lambda b, s, t: (b * steps + t, s)),
            pl.BlockSpec((tt, sw), lambda b, s, t: (b * steps + t, RNN_SUPER + s)),
            pl.BlockSpec((CONV_WIDTH, sw), lambda b, s, t: (0, s)),
            vec_spec,
            w_spec, w_spec,
            vec_spec, vec_spec, vec_spec,
        ],
        out_specs=pl.BlockSpec((tt, sw), lambda b, s, t: (b * steps + t, s)),
        out_shape=jax.ShapeDtypeStruct((m, D_RNN), BF16),
        scratch_shapes=[
            pltpu.VMEM((tt + 8, sw), F32),
            pltpu.VMEM((tt, sw), F32),
            pltpu.VMEM((tt, sw), F32),
            pltpu.VMEM((tt, sw), F32),
            pltpu.VMEM((8, sw), F32),
        ],
        compiler_params=_params(("parallel", "parallel", "arbitrary")),
        name="rglru_core",
    )(proj_out, proj_out, conv_w, vec(conv_b), wa, wx, vec(ga_b), vec(gx_b), vec(lam))


def attention_mixer(h, hb, w_in, gn_g, gn_b, w_out, ln_g, ln_b, batch, seq):
    wb = w_in.astype(BF16)
    cos_tab, sin_tab = _rotary_tables(seq)
    qk = proj_rotary(hb, wb[:, :2 * D_RET], cos_tab, sin_tab, seq)
    rg = proj(hb, wb[:, 3 * D_RET:4 * D_RET], F32)
    vm = proj(hb, jnp.concatenate([wb[:, 2 * D_RET:3 * D_RET], wb[:, 4 * D_RET:]], axis=1), BF16,
              scaled_cols=(D_RET, D_RET + D_MOBA), scale=MOBA_HEAD_DIM ** -0.5)
    ro = retention_gn_gate(qk, vm, rg, gn_g, gn_b, batch, seq)
    mo = moba(vm, batch, seq)
    wo = w_out.astype(BF16)
    return outproj_ln(h, [ro, mo], [wo[:D_RET], wo[D_RET:]], ln_g, ln_b)


def rglru_mixer(h, hb, w_in, conv_w, conv_b, ga_w, ga_b, gx_w, gx_b, lam, w_out, ln_g, ln_b, batch, seq):
    pr = proj(hb, w_in.astype(BF16), F32, tm=512, tn=D_RNN, weights_outer=True)
    y = rglru_core(pr, conv_w, conv_b, _windowed_block_diag(ga_w).astype(BF16),
                   _windowed_block_diag(gx_w).astype(BF16), ga_b, gx_b, lam, batch, seq)
    return outproj_ln(h, [y], [w_out.astype(BF16)], ln_g, ln_b)


def kernel(x, ln_g, ln_b, ffn_w_gate, ffn_w_up, ffn_w_down, attn_w_in, ret_gn_g, ret_gn_b, attn_w_out, rnn_w_in, rnn_conv_w, rnn_conv_b, rnn_gate_a_w, rnn_gate_a_b, rnn_gate_x_w, rnn_gate_x_b, rnn_lambda, rnn_w_out):
    batch, seq, d = x.shape
    h = x.reshape(batch * seq, d)
    for layer in range(DEPTH):
        ffn = lambda hh, half, ln_i, emit_bf16: ffn_ln(
            hh, ffn_w_gate[layer, half].astype(BF16), ffn_w_up[layer, half].astype(BF16),
            ffn_w_down[layer, half].astype(BF16), ln_g[layer, ln_i], ln_b[layer, ln_i], emit_bf16=emit_bf16)
        h, hb = ffn(h, 0, 0, True)
        j = layer // 2
        if layer % 2 == 0:
            h = attention_mixer(h, hb, attn_w_in[j], ret_gn_g[j], ret_gn_b[j], attn_w_out[j],
                                ln_g[layer, 1], ln_b[layer, 1], batch, seq)
        else:
            h = rglru_mixer(h, hb, rnn_w_in[j], rnn_conv_w[j], rnn_conv_b[j], rnn_gate_a_w[j], rnn_gate_a_b[j],
                            rnn_gate_x_w[j], rnn_gate_x_b[j], rnn_lambda[j], rnn_w_out[j],
                            ln_g[layer, 1], ln_b[layer, 1], batch, seq)
        (h,) = ffn(h, 1, 2, False)
    return h.reshape(batch, seq, d)
```

```python
import functools
import math

import jax
import jax.numpy as jnp
from jax import lax
from jax.experimental import pallas as pl
from jax.experimental.pallas import tpu as pltpu

F32 = jnp.float32
BF16 = jnp.bfloat16

D_MODEL = 2048
DEPTH = 2
N_RET_HEADS = 8
RET_HEAD_DIM = 128
D_RET = N_RET_HEADS * RET_HEAD_DIM
RET_KERNEL_CHUNK = 256
ROPE_BASE = 10000.0
N_MOBA_HEADS = 8
MOBA_HEAD_DIM = 128
D_MOBA = N_MOBA_HEADS * MOBA_HEAD_DIM
MOBA_BLOCK = 256
MOBA_TOPK = 3
D_RNN = 2816
N_RNN_BLOCKS = 16
RNN_BLOCK = D_RNN // N_RNN_BLOCKS
CONV_WIDTH = 4
LRU_C = 8.0
D_FF = 5632
LN_EPS = 1e-5
DEEPNORM_ALPHA = (2.0 * DEPTH) ** 0.25

RNN_SUPER = 2
RNN_SUPER_W = D_RNN // RNN_SUPER
RNN_BLOCKS_PER_SUPER = N_RNN_BLOCKS // RNN_SUPER

NEG_BIG = -1e30

VMEM_LIMIT = 56 * 1024 * 1024

NT_DIMS = (((1,), (1,)), ((), ()))
TN_DIMS = (((0,), (0,)), ((), ()))


def _params(semantics):
    return pltpu.CompilerParams(dimension_semantics=semantics, vmem_limit_bytes=VMEM_LIMIT)


def _layer_norm(y, g, b):
    mu = jnp.mean(y, axis=-1, keepdims=True)
    yc = y - mu
    var = jnp.mean(yc * yc, axis=-1, keepdims=True)
    return yc * lax.rsqrt(var + LN_EPS) * g + b


FFN_LN_CHUNK = 64


def _ffn_ln_kernel(emit_bf16, x_ref, wg_ref, wu_ref, wd_ref, g_ref, b_ref, *refs):
    if emit_bf16:
        o_ref, ob_ref, xb_ref, acc_ref, y_ref = refs
    else:
        o_ref, xb_ref, acc_ref, y_ref = refs
    i = pl.program_id(0)
    j = pl.program_id(1)
    n_tiles = pl.num_programs(0) - 1
    last = pl.num_programs(1) - 1
    n_chunks = y_ref.shape[0] // FFN_LN_CHUNK

    def ln_chunk():
        c = jnp.minimum(j, n_chunks - 1)
        rows = pl.ds(pl.multiple_of(c * FFN_LN_CHUNK, FFN_LN_CHUNK), FFN_LN_CHUNK)
        out = _layer_norm(y_ref[rows, :], g_ref[...], b_ref[...])
        o_ref[rows, :] = out
        if emit_bf16:
            ob_ref[rows, :] = out.astype(BF16)

    @pl.when(jnp.logical_and(i == 0, j == 0))
    def _():
        y_ref[...] = jnp.zeros_like(y_ref)

    @pl.when(jnp.logical_and(i < n_tiles, j == 0))
    def _():
        xb_ref[...] = x_ref[...].astype(BF16)
        acc_ref[...] = jnp.zeros_like(acc_ref)

    @pl.when(i < n_tiles)
    def _():
        xb = xb_ref[...]
        gate = jnp.dot(xb, wg_ref[...], preferred_element_type=F32)
        up = jnp.dot(xb, wu_ref[...], preferred_element_type=F32)
        act = (gate * jax.nn.sigmoid(gate) * up).astype(BF16)
        acc_ref[...] += jnp.dot(act, wd_ref[...], preferred_element_type=F32)
        ln_chunk()

    @pl.when(i == n_tiles)
    def _():
        ln_chunk()

    @pl.when(jnp.logical_and(i < n_tiles, j == last))
    def _():
        y_ref[...] = DEEPNORM_ALPHA * x_ref[...] + 0.5 * acc_ref[...]


def ffn_ln(x, wg, wu, wd, ln_g, ln_b, *, emit_bf16, tm=512, tf=512):
    m, d = x.shape
    f = wg.shape[1]
    n_tiles, n_f = m // tm, f // tf
    assert tm // FFN_LN_CHUNK <= n_f
    cur_tile = lambda i, j: (jnp.minimum(i, n_tiles - 1), 0)
    prev_tile = lambda i, j: (jnp.maximum(i - 1, 0), 0)
    f_step = lambda i, j: jnp.where(i == n_tiles, n_f - 1, j)
    out_specs = [pl.BlockSpec((tm, d), prev_tile)]
    out_shape = [jax.ShapeDtypeStruct((m, d), F32)]
    if emit_bf16:
        out_specs.append(pl.BlockSpec((tm, d), prev_tile))
        out_shape.append(jax.ShapeDtypeStruct((m, d), BF16))
    return pl.pallas_call(
        functools.partial(_ffn_ln_kernel, emit_bf16),
        grid=(n_tiles + 1, n_f),
        in_specs=[
            pl.BlockSpec((tm, d), cur_tile),
            pl.BlockSpec((d, tf), lambda i, j: (0, f_step(i, j))),
            pl.BlockSpec((d, tf), lambda i, j: (0, f_step(i, j))),
            pl.BlockSpec((tf, d), lambda i, j: (f_step(i, j), 0)),
            pl.BlockSpec((1, d), lambda i, j: (0, 0)),
            pl.BlockSpec((1, d), lambda i, j: (0, 0)),
        ],
        out_specs=out_specs,
        out_shape=out_shape,
        scratch_shapes=[pltpu.VMEM((tm, d), BF16), pltpu.VMEM((tm, d), F32), pltpu.VMEM((tm, d), F32)],
        compiler_params=_params(("arbitrary", "arbitrary")),
        name="ffn_ln",
    )(x, wg, wu, wd, ln_g.reshape(1, d), ln_b.reshape(1, d))


def _proj_kernel(scaled_tiles, scale, col_axis, x_ref, w_ref, o_ref):
    acc = jnp.dot(x_ref[...], w_ref[...], preferred_element_type=F32)
    if scaled_tiles is not None:
        j = pl.program_id(col_axis)
        lo, hi = scaled_tiles
        acc = acc * jnp.where(jnp.logical_and(j >= lo, j < hi), scale, 1.0).astype(F32)
    o_ref[...] = acc.astype(o_ref.dtype)


def _proj_rotary_kernel(x_ref, w_ref, cos_ref, sin_ref, o_ref):
    acc = jnp.dot(x_ref[...], w_ref[...], preferred_element_type=F32)
    cos = cos_ref[0]
    sin = sin_ref[0]
    hd = cos.shape[-1]
    for c in range(acc.shape[1] // hd):
        t = acc[:, c * hd:(c + 1) * hd]
        rot = pltpu.roll(t, hd // 2, 1)
        o_ref[:, c * hd:(c + 1) * hd] = (t * cos + rot * sin).astype(o_ref.dtype)


def proj(x, w, out_dtype, *, tm=1024, tn=1024, weights_outer=False, scaled_cols=None, scale=1.0):
    m, k = x.shape
    n = w.shape[1]
    scaled_tiles = None if scaled_cols is None else (scaled_cols[0] // tn, scaled_cols[1] // tn)
    if weights_outer:
        grid = (n // tn, m // tm)
        row = lambda j, i: (i, 0)
        col = lambda j, i: (0, j)
        out = lambda j, i: (i, j)
    else:
        grid = (m // tm, n // tn)
        row = lambda i, j: (i, 0)
        col = lambda i, j: (0, j)
        out = lambda i, j: (i, j)
    return pl.pallas_call(
        functools.partial(_proj_kernel, scaled_tiles, scale, 0 if weights_outer else 1),
        grid=grid,
        in_specs=[pl.BlockSpec((tm, k), row), pl.BlockSpec((k, tn), col)],
        out_specs=pl.BlockSpec((tm, tn), out),
        out_shape=jax.ShapeDtypeStruct((m, n), out_dtype),
        compiler_params=_params(("parallel", "arbitrary")),
        name="proj",
    )(x, w)


def proj_rotary(x, w, cos_tab, sin_tab, seq, *, tm=1024, tn=1024):
    m, k = x.shape
    n = w.shape[1]
    hd = cos_tab.shape[-1]
    tm = min(tm, seq)
    per_tab = (n // 2) // tn
    tiles_per_seq = seq // tm
    tab_spec = pl.BlockSpec((1, tm, hd), lambda i, j: (j // per_tab, i % tiles_per_seq, 0))
    return pl.pallas_call(
        _proj_rotary_kernel,
        grid=(m // tm, n // tn),
        in_specs=[
            pl.BlockSpec((tm, k), lambda i, j: (i, 0)),
            pl.BlockSpec((k, tn), lambda i, j: (0, j)),
            tab_spec,
            tab_spec,
        ],
        out_specs=pl.BlockSpec((tm, tn), lambda i, j: (i, j)),
        out_shape=jax.ShapeDtypeStruct((m, n), BF16),
        compiler_params=_params(("parallel", "arbitrary")),
        name="proj_rotary",
    )(x, w, cos_tab, sin_tab)


def _rotary_tables(seq):
    half = RET_HEAD_DIM // 2
    inv_freq = ROPE_BASE ** (-jnp.arange(half, dtype=F32) / half)
    ang = jnp.arange(seq, dtype=F32)[:, None] * inv_freq[None, :]
    cos, sin = jnp.cos(ang), jnp.sin(ang)
    cos2 = jnp.concatenate([cos, cos], axis=-1)
    sin2 = jnp.concatenate([-sin, sin], axis=-1)
    kscale = RET_HEAD_DIM ** -0.5
    return jnp.stack([cos2, cos2 * kscale]), jnp.stack([sin2, sin2 * kscale])


def _retention_kernel(q_ref, k_ref, v_ref, g_ref, intra_ref, kdec_ref, qdec_ref, cdec_ref,
                      gng_ref, gnb_ref, o_ref, state_ref):
    @pl.when(pl.program_id(2) == 0)
    def _():
        state_ref[...] = jnp.zeros_like(state_ref)

    intra = intra_ref[0]
    kdec = kdec_ref[0]
    qdec = qdec_ref[0]
    cdec = cdec_ref[0]
    gng = gng_ref[...]
    gnb = gnb_ref[...]
    c_len = intra.shape[0]
    state = state_ref[...]
    for c in range(q_ref.shape[0] // c_len):
        rows = slice(c * c_len, (c + 1) * c_len)
        q = q_ref[rows, :]
        k = k_ref[rows, :]
        v = v_ref[rows, :]
        scores = lax.dot_general(q, k, NT_DIMS, preferred_element_type=F32) * intra
        inner = jnp.dot(scores.astype(BF16), v, preferred_element_type=F32)
        cross = jnp.dot(q, state.astype(BF16), preferred_element_type=F32) * qdec
        out = inner + cross
        k_decayed = (k.astype(F32) * kdec).astype(BF16)
        kv = lax.dot_general(k_decayed, v, TN_DIMS, preferred_element_type=F32)
        state = state * cdec + kv
        mu = jnp.mean(out, axis=-1, keepdims=True)
        oc = out - mu
        var = jnp.mean(oc * oc, axis=-1, keepdims=True)
        normed = oc * lax.rsqrt(var + LN_EPS) * gng + gnb
        gate = g_ref[rows, :]
        o_ref[rows, :] = (gate * jax.nn.sigmoid(gate) * normed).astype(o_ref.dtype)
    state_ref[...] = state


def _retention_tables(c_len):
    h = N_RET_HEADS
    hd = RET_HEAD_DIM
    log_g = jnp.log1p(-jnp.exp2(-5.0 - jnp.arange(h, dtype=F32)))
    c = jnp.arange(c_len, dtype=F32)
    rel = c[:, None] - c[None, :]
    intra = jnp.where(rel >= 0, jnp.exp(jnp.maximum(rel, 0.0)[None] * log_g[:, None, None]), 0.0)
    k_dec = jnp.exp((c_len - 1 - c)[None, :] * log_g[:, None])
    q_dec = jnp.exp((c + 1)[None, :] * log_g[:, None])
    chunk_dec = jnp.exp(c_len * log_g)
    return (intra.astype(F32),
            jnp.broadcast_to(k_dec[:, :, None], (h, c_len, hd)),
            jnp.broadcast_to(q_dec[:, :, None], (h, c_len, hd)),
            jnp.broadcast_to(chunk_dec[:, None, None], (h, hd, hd)))


def retention_gn_gate(qk, vm, rg, gn_g, gn_b, batch, seq, *, rows=1024):
    m = qk.shape[0]
    h = N_RET_HEADS
    hd = RET_HEAD_DIM
    rows = min(rows, seq)
    steps = seq // rows
    tables = _retention_tables(min(RET_KERNEL_CHUNK, rows))
    intra, kdec, qdec, cdec = tables
    row_map = lambda off: (lambda b, hh, t: (b * steps + t, off + hh))
    tab_specs = [pl.BlockSpec((1,) + tab.shape[1:], lambda b, hh, t: (hh, 0, 0)) for tab in tables]
    vec_spec = pl.BlockSpec((1, hd), lambda b, hh, t: (0, hh))
    return pl.pallas_call(
        _retention_kernel,
        grid=(batch, h, steps),
        in_specs=[
            pl.BlockSpec((rows, hd), row_map(0)),
            pl.BlockSpec((rows, hd), row_map(h)),
            pl.BlockSpec((rows, hd), row_map(0)),
            pl.BlockSpec((rows, hd), row_map(0)),
            *tab_specs,
            vec_spec, vec_spec,
        ],
        out_specs=pl.BlockSpec((rows, hd), row_map(0)),
        out_shape=jax.ShapeDtypeStruct((m, h * hd), BF16),
        scratch_shapes=[pltpu.VMEM((hd, hd), F32)],
        compiler_params=_params(("parallel", "parallel", "arbitrary")),
        name="retention",
    )(qk, qk, vm, rg, intra, kdec, qdec, cdec, gn_g.reshape(1, h * hd), gn_b.reshape(1, h * hd))


def _moba_kernel(blk, q_ref, k_ref, v_ref, o_ref, kaug_ref, vt_ref, kmean_ref, qaug_ref, s_ref, smax_ref,
                 m_ref, l_ref, acc_ref):
    i = pl.program_id(2)
    nb, hd = kmean_ref.shape
    n_tiles, _, tile = vt_ref.shape
    per_tile = tile // blk

    @pl.when(i == 0)
    def _():
        lane = lax.broadcasted_iota(jnp.int32, (blk, hd), 1)
        for j in range(nb):
            rows = slice(j * blk, (j + 1) * blk)
            kb = k_ref[rows, :]
            kmean_ref[j:j + 1, :] = jnp.mean(kb.astype(F32), axis=0, keepdims=True)
            kaug_ref[rows, 0:hd] = kb
            kaug_ref[rows, hd:2 * hd] = jnp.where(lane == j, 1.0, 0.0).astype(BF16)
        for g in range(n_tiles):
            vt_ref[g] = v_ref[g * tile:(g + 1) * tile, :].T

    q = q_ref[...]

    gate = lax.dot_general(kmean_ref[...].astype(BF16), q, NT_DIMS, preferred_element_type=F32)
    n_idx = lax.broadcasted_iota(jnp.int32, gate.shape, 0)
    q_blk = i * per_tile + lax.broadcasted_iota(jnp.int32, gate.shape, 1) // blk
    gate = jnp.where(n_idx < q_blk, gate, -jnp.inf)
    n_f = n_idx.astype(F32)
    bias = jnp.where(n_idx == q_blk, 0.0, NEG_BIG).astype(F32)
    for _ in range(min(MOBA_TOPK, nb)):
        best = jnp.max(gate, axis=0, keepdims=True)
        first = jnp.min(jnp.where(gate == best, n_f, float(nb)), axis=0, keepdims=True)
        hit = jnp.logical_and(n_f == first, best > -jnp.inf)
        bias = jnp.where(hit, 0.0, bias)
        gate = jnp.where(hit, -jnp.inf, gate)
    bias_rows = jnp.concatenate([bias, jnp.zeros((hd - nb, tile), F32)], axis=0)
    qaug_ref[:, 0:hd] = q
    qaug_ref[:, hd:2 * hd] = bias_rows.T.astype(BF16)

    def produce(slot, g):
        start = pl.multiple_of(g * tile, tile)
        s = lax.dot_general(kaug_ref[pl.ds(start, tile), :], qaug_ref[...], NT_DIMS,
                            preferred_element_type=F32)
        s_ref[slot] = s
        smax_ref[slot] = jnp.max(s, axis=0, keepdims=True)

    m_ref[...] = jnp.full(m_ref.shape, NEG_BIG, F32)
    l_ref[...] = jnp.zeros_like(l_ref)
    acc_ref[...] = jnp.zeros_like(acc_ref)

    def consume(g, s_g, s_max):
        m_run = m_ref[...]
        m_new = jnp.maximum(m_run, s_max)
        corr = jnp.exp(m_run - m_new)
        p_g = jnp.exp(s_g - m_new)
        m_ref[...] = m_new
        l_ref[...] = l_ref[...] * corr + jnp.sum(p_g, axis=0, keepdims=True)
        acc_ref[...] = acc_ref[...] * corr + jnp.dot(vt_ref[g], p_g.astype(BF16), preferred_element_type=F32)

    def consume_own(slot):
        s = s_ref[slot]
        kpos = lax.broadcasted_iota(jnp.int32, s.shape, 0)
        qpos = lax.broadcasted_iota(jnp.int32, s.shape, 1)
        s = jnp.where(kpos <= qpos, s, NEG_BIG)
        consume(i, s, jnp.max(s, axis=0, keepdims=True))

    produce(0, 0)

    def two_tiles(t, carry):
        g = 2 * t
        consume(g, s_ref[0], smax_ref[0])
        produce(1, g + 1)
        consume(g + 1, s_ref[1], smax_ref[1])
        produce(0, g + 2)
        return carry

    lax.fori_loop(0, i // 2, two_tiles, 0)

    @pl.when(i % 2 == 1)
    def _():
        consume(i - 1, s_ref[0], smax_ref[0])
        produce(1, i)
        consume_own(1)

    @pl.when(i % 2 == 0)
    def _():
        consume_own(0)

    o_ref[...] = (acc_ref[...] / l_ref[...]).T.astype(o_ref.dtype)


def moba(vm, batch, seq, *, blocks_per_tile=4):
    m = vm.shape[0]
    h = N_MOBA_HEADS
    hd = MOBA_HEAD_DIM
    blk = MOBA_BLOCK
    nb = seq // blk
    per_tile = min(blocks_per_tile, nb)
    tile = per_tile * blk
    n_tiles = nb // per_tile
    assert nb % per_tile == 0 and nb <= hd
    return pl.pallas_call(
        functools.partial(_moba_kernel, blk),
        grid=(batch, h, n_tiles),
        in_specs=[
            pl.BlockSpec((tile, hd), lambda b, hh, i: (b * n_tiles + i, h + hh)),
            pl.BlockSpec((seq, hd), lambda b, hh, i: (b, 2 * h + hh)),
            pl.BlockSpec((seq, hd), lambda b, hh, i: (b, 3 * h + hh)),
        ],
        out_specs=pl.BlockSpec((tile, hd), lambda b, hh, i: (b * n_tiles + i, hh)),
        out_shape=jax.ShapeDtypeStruct((m, h * hd), BF16),
        scratch_shapes=[
            pltpu.VMEM((seq, 2 * hd), BF16),
            pltpu.VMEM((n_tiles, hd, tile), BF16),
            pltpu.VMEM((nb, hd), F32),
            pltpu.VMEM((tile, 2 * hd), BF16),
            pltpu.VMEM((2, tile, tile), F32),
            pltpu.VMEM((2, 1, tile), F32),
            pltpu.VMEM((1, tile), F32),
            pltpu.VMEM((1, tile), F32),
            pltpu.VMEM((hd, tile), F32),
        ],
        compiler_params=_params(("parallel", "parallel", "arbitrary")),
        name="moba",
    )(vm, vm, vm)


def _outproj_ln_kernel(n_in, *refs):
    h_ref = refs[0]
    a_refs = refs[1:1 + n_in]
    w_refs = refs[1 + n_in:1 + 2 * n_in]
    g_ref, b_ref, o_ref = refs[1 + 2 * n_in:]
    for r in range(0, o_ref.shape[0], OUTPROJ_SUB_ROWS):
        rows = slice(r, r + OUTPROJ_SUB_ROWS)
        acc = jnp.dot(a_refs[0][rows, :], w_refs[0][...], preferred_element_type=F32)
        for a_ref, w_ref in zip(a_refs[1:], w_refs[1:]):
            acc += jnp.dot(a_ref[rows, :], w_ref[...], preferred_element_type=F32)
        y = DEEPNORM_ALPHA * h_ref[rows, :] + acc
        o_ref[rows, :] = _layer_norm(y, g_ref[...], b_ref[...])


OUTPROJ_SUB_ROWS = 256


def outproj_ln(h, acts, weights, ln_g, ln_b, *, tm=512):
    m, d = h.shape
    n_in = len(acts)
    row_spec = pl.BlockSpec((tm, d), lambda i: (i, 0))
    in_specs = [row_spec]
    in_specs += [pl.BlockSpec((tm, a.shape[1]), lambda i: (i, 0)) for a in acts]
    in_specs += [pl.BlockSpec(w.shape, lambda i: (0, 0)) for w in weights]
    in_specs += [pl.BlockSpec((1, d), lambda i: (0, 0))] * 2
    return pl.pallas_call(
        functools.partial(_outproj_ln_kernel, n_in),
        grid=(m // tm,),
        in_specs=in_specs,
        out_specs=row_spec,
        out_shape=jax.ShapeDtypeStruct((m, d), F32),
        compiler_params=_params(("parallel",)),
        name="outproj_ln",
    )(h, *acts, *weights, ln_g.reshape(1, d), ln_b.reshape(1, d))


def _gelu_tanh(x):
    return 0.5 * x * (1.0 + jnp.tanh(math.sqrt(2.0 / math.pi) * (x + 0.044715 * (x * x * x))))


def _softplus(x):
    return jnp.maximum(x, 0.0) + jnp.log1p(jnp.exp(-jnp.abs(x)))


def _rglru_kernel(gate_ref, xr_ref, cw_ref, cb_ref, wa_ref, wx_ref, gab_ref, gxb_ref, lam_ref, o_ref,
                  xext_ref, a_ref, b_ref, hs_ref, carry_ref):
    tt = xr_ref.shape[0]
    halo = 8

    @pl.when(pl.program_id(2) == 0)
    def _():
        xext_ref[0:halo, :] = jnp.zeros((halo, xext_ref.shape[1]), F32)
        carry_ref[...] = jnp.zeros_like(carry_ref)

    xext_ref[halo:halo + tt, :] = xr_ref[...]
    cw = cw_ref[...]
    u = cb_ref[...]
    for tap in range(CONV_WIDTH):
        off = halo - (CONV_WIDTH - 1) + tap
        u = u + cw[tap:tap + 1, :] * xext_ref[off:off + tt, :]
    xext_ref[0:halo, :] = xext_ref[tt:tt + halo, :]

    ub = u.astype(BF16)
    sw = ub.shape[1]

    def gate_pre(w_ref):
        parts = [jnp.dot(ub[:, ks:ks + RNN_GATE_WINDOW], w_ref[0, n], preferred_element_type=F32)
                 for n, ks in enumerate(RNN_GATE_WINDOW_STARTS)]
        return jnp.concatenate(parts, axis=1)[:, :sw]

    r = jax.nn.sigmoid(gate_pre(wa_ref) + gab_ref[...])
    gi = jax.nn.sigmoid(gate_pre(wx_ref) + gxb_ref[...])
    log_a = -LRU_C * r * _softplus(-lam_ref[...])
    a = jnp.exp(log_a)
    a_ref[...] = a
    b_ref[...] = jnp.sqrt(1.0 - a * a) * (gi * u)

    def step(t, h):
        h = a_ref[pl.ds(t, 1), :] * h + b_ref[pl.ds(t, 1), :]
        hs_ref[pl.ds(t, 1), :] = h
        return h

    h_last = lax.fori_loop(0, tt, step, carry_ref[0:1, :], unroll=8)
    carry_ref[0:1, :] = h_last
    o_ref[...] = (hs_ref[...] * _gelu_tanh(gate_ref[...])).astype(o_ref.dtype)


def _gate_windows():
    spans = []
    for lo in range(0, RNN_SUPER_W, RNN_GATE_TILE):
        hi = min(lo + RNN_GATE_TILE, RNN_SUPER_W)
        first, last = lo // RNN_BLOCK, (hi - 1) // RNN_BLOCK
        spans.append((first * RNN_BLOCK // 128 * 128, -(-(last + 1) * RNN_BLOCK // 128) * 128))
    width = max(e - s for s, e in spans)
    return width, tuple(min(s, RNN_SUPER_W - width) for s, _ in spans)


RNN_GATE_TILE = 256
RNN_GATE_WINDOW, RNN_GATE_WINDOW_STARTS = _gate_windows()


def _windowed_block_diag(w):
    n, bs = RNN_BLOCKS_PER_SUPER, RNN_BLOCK
    w = w.reshape(RNN_SUPER, n, bs, bs)
    eye = jnp.eye(n, dtype=w.dtype)
    dense = jnp.einsum('sgij,gh->sgihj', w, eye).reshape(RNN_SUPER, n * bs, n * bs)
    n_tiles = len(RNN_GATE_WINDOW_STARTS)
    dense = jnp.pad(dense, ((0, 0), (0, 0), (0, n_tiles * RNN_GATE_TILE - RNN_SUPER_W)))
    tiles = [dense[:, ks:ks + RNN_GATE_WINDOW, t * RNN_GATE_TILE:(t + 1) * RNN_GATE_TILE]
             for t, ks in enumerate(RNN_GATE_WINDOW_STARTS)]
    return jnp.stack(tiles, axis=1)


def rglru_core(proj_out, conv_w, conv_b, wa, wx, ga_b, gx_b, lam, batch, seq, *, tt=512):
    m = proj_out.shape[0]
    sw = RNN_SUPER_W
    tt = min(tt, seq)
    steps = seq // tt
    vec = lambda v: v.reshape(1, D_RNN)
    vec_spec = pl.BlockSpec((1, sw), lambda b, s, t: (0, s))
    w_spec = pl.BlockSpec((1,) + wa.shape[1:], lambda b, s, t: (s, 0, 0, 0))
    return pl.pallas_call(
        _rglru_kernel,
        grid=(batch, RNN_SUPER, steps),
        in_specs=[
            pl.BlockSpec((tt, sw), lambda b, s, t: (b * steps + t, s)),
            pl.BlockSpec((tt, sw), lambda b, s, t: (b * steps + t, RNN_SUPER + s)),
            pl.BlockSpec((CONV_WIDTH, sw), lambda b, s, t: (0, s)),
            vec_spec,
            w_spec, w_spec,
            vec_spec, vec_spec, vec_spec,
        ],
        out_specs=pl.BlockSpec((tt, sw), lambda b, s, t: (b * steps + t, s)),
        out_shape=jax.ShapeDtypeStruct((m, D_RNN), BF16),
        scratch_shapes=[
            pltpu.VMEM((tt + 8, sw), F32),
            pltpu.VMEM((tt, sw), F32),
            pltpu.VMEM((tt, sw), F32),
            pltpu.VMEM((tt, sw), F32),
            pltpu.VMEM((8, sw), F32),
        ],
        compiler_params=_params(("parallel", "parallel", "arbitrary")),
        name="rglru_core",
    )(proj_out, proj_out, conv_w, vec(conv_b), wa, wx, vec(ga_b), vec(gx_b), vec(lam))


def attention_mixer(h, hb, w_in, gn_g, gn_b, w_out, ln_g, ln_b, batch, seq):
    wb = w_in.astype(BF16)
    cos_tab, sin_tab = _rotary_tables(seq)
    qk = proj_rotary(hb, wb[:, :2 * D_RET], cos_tab, sin_tab, seq)
    rg = proj(hb, wb[:, 3 * D_RET:4 * D_RET], F32)
    vm = proj(hb, jnp.concatenate([wb[:, 2 * D_RET:3 * D_RET], wb[:, 4 * D_RET:]], axis=1), BF16,
              scaled_cols=(D_RET, D_RET + D_MOBA), scale=MOBA_HEAD_DIM ** -0.5)
    ro = retention_gn_gate(qk, vm, rg, gn_g, gn_b, batch, seq)
    mo = moba(vm, batch, seq)
    wo = w_out.astype(BF16)
    return outproj_ln(h, [ro, mo], [wo[:D_RET], wo[D_RET:]], ln_g, ln_b)


def rglru_mixer(h, hb, w_in, conv_w, conv_b, ga_w, ga_b, gx_w, gx_b, lam, w_out, ln_g, ln_b, batch, seq):
    pr = proj(hb, w_in.astype(BF16), F32, tm=512, tn=D_RNN, weights_outer=True)
    y = rglru_core(pr, conv_w, conv_b, _windowed_block_diag(ga_w).astype(BF16),
                   _windowed_block_diag(gx_w).astype(BF16), ga_b, gx_b, lam, batch, seq)
    return outproj_ln(h, [y], [w_out.astype(BF16)], ln_g, ln_b)


def kernel(x, ln_g, ln_b, ffn_w_gate, ffn_w_up, ffn_w_down, attn_w_in, ret_gn_g, ret_gn_b, attn_w_out, rnn_w_in, rnn_conv_w, rnn_conv_b, rnn_gate_a_w, rnn_gate_a_b, rnn_gate_x_w, rnn_gate_x_b, rnn_lambda, rnn_w_out):
    batch, seq, d = x.shape
    h = x.reshape(batch * seq, d)
    for layer in range(DEPTH):
        ffn = lambda hh, half, ln_i, emit_bf16: ffn_ln(
            hh, ffn_w_gate[layer, half].astype(BF16), ffn_w_up[layer, half].astype(BF16),
            ffn_w_down[layer, half].astype(BF16), ln_g[layer, ln_i], ln_b[layer, ln_i], emit_bf16=emit_bf16)
        h, hb = ffn(h, 0, 0, True)
        j = layer // 2
        if layer % 2 == 0:
            h = attention_mixer(h, hb, attn_w_in[j], ret_gn_g[j], ret_gn_b[j], attn_w_out[j],
                                ln_g[layer, 1], ln_b[layer, 1], batch, seq)
        else:
            h = rglru_mixer(h, hb, rnn_w_in[j], rnn_conv_w[j], rnn_conv_b[j], rnn_gate_a_w[j], rnn_gate_a_b[j],
                            rnn_gate_x_w[j], rnn_gate_x_b[j], rnn_lambda[j], rnn_w_out[j],
                            ln_g[layer, 1], ln_b[layer, 1], batch, seq)
        (h,) = ffn(h, 1, 2, False)
    return h.reshape(batch, seq, d)
```

```python
import functools
import math

import jax
import jax.numpy as jnp
from jax import lax
from jax.experimental import pallas as pl
from jax.experimental.pallas import tpu as pltpu

F32 = jnp.float32
BF16 = jnp.bfloat16

D_MODEL = 2048
DEPTH = 2
N_RET_HEADS = 8
RET_HEAD_DIM = 128
D_RET = N_RET_HEADS * RET_HEAD_DIM
RET_KERNEL_CHUNK = 256
ROPE_BASE = 10000.0
N_MOBA_HEADS = 8
MOBA_HEAD_DIM = 128
D_MOBA = N_MOBA_HEADS * MOBA_HEAD_DIM
MOBA_BLOCK = 256
MOBA_TOPK = 3
D_RNN = 2816
N_RNN_BLOCKS = 16
RNN_BLOCK = D_RNN // N_RNN_BLOCKS
CONV_WIDTH = 4
LRU_C = 8.0
D_FF = 5632
LN_EPS = 1e-5
DEEPNORM_ALPHA = (2.0 * DEPTH) ** 0.25

RNN_SUPER = 2
RNN_SUPER_W = D_RNN // RNN_SUPER
RNN_BLOCKS_PER_SUPER = N_RNN_BLOCKS // RNN_SUPER

NEG_BIG = -1e30

VMEM_LIMIT = 56 * 1024 * 1024

NT_DIMS = (((1,), (1,)), ((), ()))
TN_DIMS = (((0,), (0,)), ((), ()))


def _params(semantics):
    return pltpu.CompilerParams(dimension_semantics=semantics, vmem_limit_bytes=VMEM_LIMIT)


def _layer_norm(y, g, b):
    mu = jnp.mean(y, axis=-1, keepdims=True)
    yc = y - mu
    var = jnp.mean(yc * yc, axis=-1, keepdims=True)
    return yc * lax.rsqrt(var + LN_EPS) * g + b


FFN_LN_CHUNK = 64


def _ffn_ln_kernel(emit_bf16, x_ref, wg_ref, wu_ref, wd_ref, g_ref, b_ref, *refs):
    if emit_bf16:
        o_ref, ob_ref, xb_ref, acc_ref, y_ref = refs
    else:
        o_ref, xb_ref, acc_ref, y_ref = refs
    i = pl.program_id(0)
    j = pl.program_id(1)
    n_tiles = pl.num_programs(0) - 1
    last = pl.num_programs(1) - 1
    n_chunks = y_ref.shape[0] // FFN_LN_CHUNK

    def ln_chunk():
        c = jnp.minimum(j, n_chunks - 1)
        rows = pl.ds(pl.multiple_of(c * FFN_LN_CHUNK, FFN_LN_CHUNK), FFN_LN_CHUNK)
        out = _layer_norm(y_ref[rows, :], g_ref[...], b_ref[...])
        o_ref[rows, :] = out
        if emit_bf16:
            ob_ref[rows, :] = out.astype(BF16)

    @pl.when(jnp.logical_and(i == 0, j == 0))
    def _():
        y_ref[...] = jnp.zeros_like(y_ref)

    @pl.when(jnp.logical_and(i < n_tiles, j == 0))
    def _():
        xb_ref[...] = x_ref[...].astype(BF16)
        acc_ref[...] = jnp.zeros_like(acc_ref)

    @pl.when(i < n_tiles)
    def _():
        xb = xb_ref[...]
        gate = jnp.dot(xb, wg_ref[...], preferred_element_type=F32)
        up = jnp.dot(xb, wu_ref[...], preferred_element_type=F32)
        act = (gate * jax.nn.sigmoid(gate) * up).astype(BF16)
        acc_ref[...] += jnp.dot(act, wd_ref[...], preferred_element_type=F32)
        ln_chunk()

    @pl.when(i == n_tiles)
    def _():
        ln_chunk()

    @pl.when(jnp.logical_and(i < n_tiles, j == last))
    def _():
        y_ref[...] = DEEPNORM_ALPHA * x_ref[...] + 0.5 * acc_ref[...]


def ffn_ln(x, wg, wu, wd, ln_g, ln_b, *, emit_bf16, tm=512, tf=512):
    m, d = x.shape
    f = wg.shape[1]
    n_tiles, n_f = m // tm, f // tf
    assert tm // FFN_LN_CHUNK <= n_f
    cur_tile = lambda i, j: (jnp.minimum(i, n_tiles - 1), 0)
    prev_tile = lambda i, j: (jnp.maximum(i - 1, 0), 0)
    f_step = lambda i, j: jnp.where(i == n_tiles, n_f - 1, j)
    out_specs = [pl.BlockSpec((tm, d), prev_tile)]
    out_shape = [jax.ShapeDtypeStruct((m, d), F32)]
    if emit_bf16:
        out_specs.append(pl.BlockSpec((tm, d), prev_tile))
        out_shape.append(jax.ShapeDtypeStruct((m, d), BF16))
    return pl.pallas_call(
        functools.partial(_ffn_ln_kernel, emit_bf16),
        grid=(n_tiles + 1, n_f),
        in_specs=[
            pl.BlockSpec((tm, d), cur_tile),
            pl.BlockSpec((d, tf), lambda i, j: (0, f_step(i, j))),
            pl.BlockSpec((d, tf), lambda i, j: (0, f_step(i, j))),
            pl.BlockSpec((tf, d), lambda i, j: (f_step(i, j), 0)),
            pl.BlockSpec((1, d), lambda i, j: (0, 0)),
            pl.BlockSpec((1, d), lambda i, j: (0, 0)),
        ],
        out_specs=out_specs,
        out_shape=out_shape,
        scratch_shapes=[pltpu.VMEM((tm, d), BF16), pltpu.VMEM((tm, d), F32), pltpu.VMEM((tm, d), F32)],
        compiler_params=_params(("arbitrary", "arbitrary")),
        name="ffn_ln",
    )(x, wg, wu, wd, ln_g.reshape(1, d), ln_b.reshape(1, d))


def _proj_kernel(scaled_tiles, scale, col_axis, x_ref, w_ref, o_ref):
    acc = jnp.dot(x_ref[...], w_ref[...], preferred_element_type=F32)
    if scaled_tiles is not None:
        j = pl.program_id(col_axis)
        lo, hi = scaled_tiles
        acc = acc * jnp.where(jnp.logical_and(j >= lo, j < hi), scale, 1.0).astype(F32)
    o_ref[...] = acc.astype(o_ref.dtype)


def _proj_rotary_kernel(x_ref, w_ref, cos_ref, sin_ref, o_ref):
    acc = jnp.dot(x_ref[...], w_ref[...], preferred_element_type=F32)
    cos = cos_ref[0]
    sin = sin_ref[0]
    hd = cos.shape[-1]
    for c in range(acc.shape[1] // hd):
        t = acc[:, c * hd:(c + 1) * hd]
        rot = pltpu.roll(t, hd // 2, 1)
        o_ref[:, c * hd:(c + 1) * hd] = (t * cos + rot * sin).astype(o_ref.dtype)


def proj(x, w, out_dtype, *, tm=1024, tn=1024, weights_outer=False, scaled_cols=None, scale=1.0):
    m, k = x.shape
    n = w.shape[1]
    scaled_tiles = None if scaled_cols is None else (scaled_cols[0] // tn, scaled_cols[1] // tn)
    if weights_outer:
        grid = (n // tn, m // tm)
        row = lambda j, i: (i, 0)
        col = lambda j, i: (0, j)
        out = lambda j, i: (i, j)
    else:
        grid = (m // tm, n // tn)
        row = lambda i, j: (i, 0)
        col = lambda i, j: (0, j)
        out = lambda i, j: (i, j)
    return pl.pallas_call(
        functools.partial(_proj_kernel, scaled_tiles, scale, 0 if weights_outer else 1),
        grid=grid,
        in_specs=[pl.BlockSpec((tm, k), row), pl.BlockSpec((k, tn), col)],
        out_specs=pl.BlockSpec((tm, tn), out),
        out_shape=jax.ShapeDtypeStruct((m, n), out_dtype),
        compiler_params=_params(("parallel", "arbitrary")),
        name="proj",
    )(x, w)


def proj_rotary(x, w, cos_tab, sin_tab, seq, *, tm=1024, tn=1024):
    m, k = x.shape
    n = w.shape[1]
    hd = cos_tab.shape[-1]
    tm = min(tm, seq)
    per_tab = (n // 2) // tn
    tiles_per_seq = seq // tm
    tab_spec = pl.BlockSpec((1, tm, hd), lambda i, j: (j // per_tab, i % tiles_per_seq, 0))
    return pl.pallas_call(
        _proj_rotary_kernel,
        grid=(m // tm, n // tn),
        in_specs=[
            pl.BlockSpec((tm, k), lambda i, j: (i, 0)),
            pl.BlockSpec((k, tn), lambda i, j: (0, j)),
            tab_spec,
            tab_spec,
        ],
        out_specs=pl.BlockSpec((tm, tn), lambda i, j: (i, j)),
        out_shape=jax.ShapeDtypeStruct((m, n), BF16),
        compiler_params=_params(("parallel", "arbitrary")),
        name="proj_rotary",
    )(x, w, cos_tab, sin_tab)


def _rotary_tables(seq):
    half = RET_HEAD_DIM // 2
    inv_freq = ROPE_BASE ** (-jnp.arange(half, dtype=F32) / half)
    ang = jnp.arange(seq, dtype=F32)[:, None] * inv_freq[None, :]
    cos, sin = jnp.cos(ang), jnp.sin(ang)
    cos2 = jnp.concatenate([cos, cos], axis=-1)
    sin2 = jnp.concatenate([-sin, sin], axis=-1)
    kscale = RET_HEAD_DIM ** -0.5
    return jnp.stack([cos2, cos2 * kscale]), jnp.stack([sin2, sin2 * kscale])


def _retention_kernel(q_ref, k_ref, v_ref, g_ref, intra_ref, kdec_ref, qdec_ref, cdec_ref,
                      gng_ref, gnb_ref, o_ref, state_ref):
    @pl.when(pl.program_id(2) == 0)
    def _():
        state_ref[...] = jnp.zeros_like(state_ref)

    intra = intra_ref[0]
    kdec = kdec_ref[0]
    qdec = qdec_ref[0]
    cdec = cdec_ref[0]
    gng = gng_ref[...]
    gnb = gnb_ref[...]
    c_len = intra.shape[0]
    state = state_ref[...]
    for c in range(q_ref.shape[0] // c_len):
        rows = slice(c * c_len, (c + 1) * c_len)
        q = q_ref[rows, :]
        k = k_ref[rows, :]
        v = v_ref[rows, :]
        scores = lax.dot_general(q, k, NT_DIMS, preferred_element_type=F32) * intra
        inner = jnp.dot(scores.astype(BF16), v, preferred_element_type=F32)
        cross = jnp.dot(q, state.astype(BF16), preferred_element_type=F32) * qdec
        out = inner + cross
        k_decayed = (k.astype(F32) * kdec).astype(BF16)
        kv = lax.dot_general(k_decayed, v, TN_DIMS, preferred_element_type=F32)
        state = state * cdec + kv
        mu = jnp.mean(out, axis=-1, keepdims=True)
        oc = out - mu
        var = jnp.mean(oc * oc, axis=-1, keepdims=True)
        normed = oc * lax.rsqrt(var + LN_EPS) * gng + gnb
        gate = g_ref[rows, :]
        o_ref[rows, :] = (gate * jax.nn.sigmoid(gate) * normed).astype(o_ref.dtype)
    state_ref[...] = state


def _retention_tables(c_len):
    h = N_RET_HEADS
    hd = RET_HEAD_DIM
    log_g = jnp.log1p(-jnp.exp2(-5.0 - jnp.arange(h, dtype=F32)))
    c = jnp.arange(c_len, dtype=F32)
    rel = c[:, None] - c[None, :]
    intra = jnp.where(rel >= 0, jnp.exp(jnp.maximum(rel, 0.0)[None] * log_g[:, None, None]), 0.0)
    k_dec = jnp.exp((c_len - 1 - c)[None, :] * log_g[:, None])
    q_dec = jnp.exp((c + 1)[None, :] * log_g[:, None])
    chunk_dec = jnp.exp(c_len * log_g)
    return (intra.astype(F32),
            jnp.broadcast_to(k_dec[:, :, None], (h, c_len, hd)),
            jnp.broadcast_to(q_dec[:, :, None], (h, c_len, hd)),
            jnp.broadcast_to(chunk_dec[:, None, None], (h, hd, hd)))


def retention_gn_gate(qk, vm, rg, gn_g, gn_b, batch, seq, *, rows=1024):
    m = qk.shape[0]
    h = N_RET_HEADS
    hd = RET_HEAD_DIM
    rows = min(rows, seq)
    steps = seq // rows
    tables = _retention_tables(min(RET_KERNEL_CHUNK, rows))
    intra, kdec, qdec, cdec = tables
    row_map = lambda off: (lambda b, hh, t: (b * steps + t, off + hh))
    tab_specs = [pl.BlockSpec((1,) + tab.shape[1:], lambda b, hh, t: (hh, 0, 0)) for tab in tables]
    vec_spec = pl.BlockSpec((1, hd), lambda b, hh, t: (0, hh))
    return pl.pallas_call(
        _retention_kernel,
        grid=(batch, h, steps),
        in_specs=[
            pl.BlockSpec((rows, hd), row_map(0)),
            pl.BlockSpec((rows, hd), row_map(h)),
            pl.BlockSpec((rows, hd), row_map(0)),
            pl.BlockSpec((rows, hd), row_map(0)),
            *tab_specs,
            vec_spec, vec_spec,
        ],
        out_specs=pl.BlockSpec((rows, hd), row_map(0)),
        out_shape=jax.ShapeDtypeStruct((m, h * hd), BF16),
        scratch_shapes=[pltpu.VMEM((hd, hd), F32)],
        compiler_params=_params(("parallel", "parallel", "arbitrary")),
        name="retention",
    )(qk, qk, vm, rg, intra, kdec, qdec, cdec, gn_g.reshape(1, h * hd), gn_b.reshape(1, h * hd))


def _moba_kernel(blk, q_ref, k_ref, v_ref, o_ref, kaug_ref, vt_ref, kmean_ref, qaug_ref, s_ref, smax_ref,
                 m_ref, l_ref, acc_ref):
    i = pl.program_id(2)
    nb, hd = kmean_ref.shape
    n_tiles, _, tile = vt_ref.shape
    per_tile = tile // blk

    @pl.when(i == 0)
    def _():
        lane = lax.broadcasted_iota(jnp.int32, (blk, hd), 1)
        for j in range(nb):
            rows = slice(j * blk, (j + 1) * blk)
            kb = k_ref[rows, :]
            kmean_ref[j:j + 1, :] = jnp.mean(kb.astype(F32), axis=0, keepdims=True)
            kaug_ref[rows, 0:hd] = kb
            kaug_ref[rows, hd:2 * hd] = jnp.where(lane == j, 1.0, 0.0).astype(BF16)
        for g in range(n_tiles):
            vt_ref[g] = v_ref[g * tile:(g + 1) * tile, :].T

    q = q_ref[...]

    gate = lax.dot_general(kmean_ref[...].astype(BF16), q, NT_DIMS, preferred_element_type=F32)
    n_idx = lax.broadcasted_iota(jnp.int32, gate.shape, 0)
    q_blk = i * per_tile + lax.broadcasted_iota(jnp.int32, gate.shape, 1) // blk
    gate = jnp.where(n_idx < q_blk, gate, -jnp.inf)
    n_f = n_idx.astype(F32)
    bias = jnp.where(n_idx == q_blk, 0.0, NEG_BIG).astype(F32)
    for _ in range(min(MOBA_TOPK, nb)):
        best = jnp.max(gate, axis=0, keepdims=True)
        first = jnp.min(jnp.where(gate == best, n_f, float(nb)), axis=0, keepdims=True)
        hit = jnp.logical_and(n_f == first, best > -jnp.inf)
        bias = jnp.where(hit, 0.0, bias)
        gate = jnp.where(hit, -jnp.inf, gate)
    bias_rows = jnp.concatenate([bias, jnp.zeros((hd - nb, tile), F32)], axis=0)
    qaug_ref[:, 0:hd] = q
    qaug_ref[:, hd:2 * hd] = bias_rows.T.astype(BF16)

    def produce(slot, g):
        start = pl.multiple_of(g * tile, tile)
        s = lax.dot_general(kaug_ref[pl.ds(start, tile), :], qaug_ref[...], NT_DIMS,
                            preferred_element_type=F32)
        s_ref[slot] = s
        smax_ref[slot] = jnp.max(s, axis=0, keepdims=True)

    m_ref[...] = jnp.full(m_ref.shape, NEG_BIG, F32)
    l_ref[...] = jnp.zeros_like(l_ref)
    acc_ref[...] = jnp.zeros_like(acc_ref)

    def consume(g, s_g, s_max):
        m_run = m_ref[...]
        m_new = jnp.maximum(m_run, s_max)
        corr = jnp.exp(m_run - m_new)
        p_g = jnp.exp(s_g - m_new)
        m_ref[...] = m_new
        l_ref[...] = l_ref[...] * corr + jnp.sum(p_g, axis=0, keepdims=True)
        acc_ref[...] = acc_ref[...] * corr + jnp.dot(vt_ref[g], p_g.astype(BF16), preferred_element_type=F32)

    def consume_own(slot):
        s = s_ref[slot]
        kpos = lax.broadcasted_iota(jnp.int32, s.shape, 0)
        qpos = lax.broadcasted_iota(jnp.int32, s.shape, 1)
        s = jnp.where(kpos <= qpos, s, NEG_BIG)
        consume(i, s, jnp.max(s, axis=0, keepdims=True))

    produce(0, 0)

    def two_tiles(t, carry):
        g = 2 * t
        consume(g, s_ref[0], smax_ref[0])
        produce(1, g + 1)
        consume(g + 1, s_ref[1], smax_ref[1])
        produce(0, g + 2)
        return carry

    lax.fori_loop(0, i // 2, two_tiles, 0)

    @pl.when(i % 2 == 1)
    def _():
        consume(i - 1, s_ref[0], smax_ref[0])
        produce(1, i)
        consume_own(1)

    @pl.when(i % 2 == 0)
    def _():
        consume_own(0)

    o_ref[...] = (acc_ref[...] / l_ref[...]).T.astype(o_ref.dtype)


def moba(vm, batch, seq, *, blocks_per_tile=4):
    m = vm.shape[0]
    h = N_MOBA_HEADS
    hd = MOBA_HEAD_DIM
    blk = MOBA_BLOCK
    nb = seq // blk
    per_tile = min(blocks_per_tile, nb)
    tile = per_tile * blk
    n_tiles = nb // per_tile
    assert nb % per_tile == 0 and nb <= hd
    return pl.pallas_call(
        functools.partial(_moba_kernel, blk),
        grid=(batch, h, n_tiles),
        in_specs=[
            pl.BlockSpec((tile, hd), lambda b, hh, i: (b * n_tiles + i, h + hh)),
            pl.BlockSpec((seq, hd), lambda b, hh, i: (b, 2 * h + hh)),
            pl.BlockSpec((seq, hd), lambda b, hh, i: (b, 3 * h + hh)),
        ],
        out_specs=pl.BlockSpec((tile, hd), lambda b, hh, i: (b * n_tiles + i, hh)),
        out_shape=jax.ShapeDtypeStruct((m, h * hd), BF16),
        scratch_shapes=[
            pltpu.VMEM((seq, 2 * hd), BF16),
            pltpu.VMEM((n_tiles, hd, tile), BF16),
            pltpu.VMEM((nb, hd), F32),
            pltpu.VMEM((tile, 2 * hd), BF16),
            pltpu.VMEM((2, tile, tile), F32),
            pltpu.VMEM((2, 1, tile), F32),
            pltpu.VMEM((1, tile), F32),
            pltpu.VMEM((1, tile), F32),
            pltpu.VMEM((hd, tile), F32),
        ],
        compiler_params=_params(("parallel", "parallel", "arbitrary")),
        name="moba",
    )(vm, vm, vm)


def _outproj_ln_kernel(n_in, *refs):
    h_ref = refs[0]
    a_refs = refs[1:1 + n_in]
    w_refs = refs[1 + n_in:1 + 2 * n_in]
    g_ref, b_ref, *o_refs = refs[1 + 2 * n_in:]
    for r in range(0, h_ref.shape[0], OUTPROJ_SUB_ROWS):
        rows = slice(r, r + OUTPROJ_SUB_ROWS)
        acc = jnp.dot(a_refs[0][rows, :], w_refs[0][...], preferred_element_type=F32)
        for a_ref, w_ref in zip(a_refs[1:], w_refs[1:]):
            acc += jnp.dot(a_ref[rows, :], w_ref[...], preferred_element_type=F32)
        y = DEEPNORM_ALPHA * h_ref[rows, :] + acc
        out = _layer_norm(y, g_ref[...], b_ref[...])
        for o_ref in o_refs:
            o_ref[rows, :] = out.astype(o_ref.dtype)


OUTPROJ_SUB_ROWS = 256


def outproj_ln(h, acts, weights, ln_g, ln_b, *, tm=512, emit_bf16=False):
    m, d = h.shape
    n_in = len(acts)
    row_spec = pl.BlockSpec((tm, d), lambda i: (i, 0))
    in_specs = [row_spec]
    in_specs += [pl.BlockSpec((tm, a.shape[1]), lambda i: (i, 0)) for a in acts]
    in_specs += [pl.BlockSpec(w.shape, lambda i: (0, 0)) for w in weights]
    in_specs += [pl.BlockSpec((1, d), lambda i: (0, 0))] * 2
    out_dtypes = (F32, BF16) if emit_bf16 else (F32,)
    outs = pl.pallas_call(
        functools.partial(_outproj_ln_kernel, n_in),
        grid=(m // tm,),
        in_specs=in_specs,
        out_specs=[row_spec] * len(out_dtypes),
        out_shape=[jax.ShapeDtypeStruct((m, d), dt) for dt in out_dtypes],
        compiler_params=_params(("parallel",)),
        name="outproj_ln",
    )(h, *acts, *weights, ln_g.reshape(1, d), ln_b.reshape(1, d))
    return outs if emit_bf16 else outs[0]


def _swiglu_up_kernel(x_ref, w_ref, o_ref):
    acc = jnp.dot(x_ref[...].astype(BF16), w_ref[0], preferred_element_type=F32)
    tf = o_ref.shape[1]
    gate = acc[:, :tf]
    up = acc[:, tf:]
    o_ref[...] = (gate * jax.nn.sigmoid(gate) * up).astype(o_ref.dtype)


def swiglu_up(x, w_gate_up, *, tm=1024):
    m, d = x.shape
    n_f, _, two_tf = w_gate_up.shape
    tf = two_tf // 2
    return pl.pallas_call(
        _swiglu_up_kernel,
        grid=(n_f, m // tm),
        in_specs=[
            pl.BlockSpec((tm, d), lambda j, i: (i, 0)),
            pl.BlockSpec((1, d, two_tf), lambda j, i: (j, 0, 0)),
        ],
        out_specs=pl.BlockSpec((tm, tf), lambda j, i: (i, j)),
        out_shape=jax.ShapeDtypeStruct((m, n_f * tf), BF16),
        compiler_params=_params(("parallel", "arbitrary")),
        name="swiglu_up",
    )(x, w_gate_up)


FFN_UP_TILE = 512
FFN_DOWN_ROWS = 256


def _gate_up_tiles(w_gate, w_up, tf):
    d, f = w_gate.shape
    both = jnp.stack([w_gate.reshape(d, f // tf, tf), w_up.reshape(d, f // tf, tf)], axis=2)
    return both.transpose(1, 0, 2, 3).reshape(f // tf, d, 2 * tf).astype(BF16)


def _gelu_tanh(x):
    return 0.5 * x * (1.0 + jnp.tanh(math.sqrt(2.0 / math.pi) * (x + 0.044715 * (x * x * x))))


def _softplus(x):
    return jnp.maximum(x, 0.0) + jnp.log1p(jnp.exp(-jnp.abs(x)))


def _rglru_kernel(gate_ref, xr_ref, cw_ref, cb_ref, wa_ref, wx_ref, gab_ref, gxb_ref, lam_ref, o_ref,
                  xext_ref, a_ref, b_ref, hs_ref, carry_ref):
    tt = xr_ref.shape[0]
    halo = 8

    @pl.when(pl.program_id(2) == 0)
    def _():
        xext_ref[0:halo, :] = jnp.zeros((halo, xext_ref.shape[1]), F32)
        carry_ref[...] = jnp.zeros_like(carry_ref)

    xext_ref[halo:halo + tt, :] = xr_ref[...]
    cw = cw_ref[...]
    u = cb_ref[...]
    for tap in range(CONV_WIDTH):
        off = halo - (CONV_WIDTH - 1) + tap
        u = u + cw[tap:tap + 1, :] * xext_ref[off:off + tt, :]
    xext_ref[0:halo, :] = xext_ref[tt:tt + halo, :]

    ub = u.astype(BF16)
    sw = ub.shape[1]

    def gate_pre(w_ref):
        parts = [jnp.dot(ub[:, ks:ks + RNN_GATE_WINDOW], w_ref[0, n], preferred_element_type=F32)
                 for n, ks in enumerate(RNN_GATE_WINDOW_STARTS)]
        return jnp.concatenate(parts, axis=1)[:, :sw]

    r = jax.nn.sigmoid(gate_pre(wa_ref) + gab_ref[...])
    gi = jax.nn.sigmoid(gate_pre(wx_ref) + gxb_ref[...])
    log_a = -LRU_C * r * _softplus(-lam_ref[...])
    a = jnp.exp(log_a)
    a_ref[...] = a
    b_ref[...] = jnp.sqrt(1.0 - a * a) * (gi * u)

    def step(t, h):
        h = a_ref[pl.ds(t, 1), :] * h + b_ref[pl.ds(t, 1), :]
        hs_ref[pl.ds(t, 1), :] = h
        return h

    h_last = lax.fori_loop(0, tt, step, carry_ref[0:1, :], unroll=8)
    carry_ref[0:1, :] = h_last
    o_ref[...] = (hs_ref[...] * _gelu_tanh(gate_ref[...])).astype(o_ref.dtype)


def _gate_windows():
    spans = []
    for lo in range(0, RNN_SUPER_W, RNN_GATE_TILE):
        hi = min(lo + RNN_GATE_TILE, RNN_SUPER_W)
        first, last = lo // RNN_BLOCK, (hi - 1) // RNN_BLOCK
        spans.append((first * RNN_BLOCK // 128 * 128, -(-(last + 1) * RNN_BLOCK // 128) * 128))
    width = max(e - s for s, e in spans)
    return width, tuple(min(s, RNN_SUPER_W - width) for s, _ in spans)


RNN_GATE_TILE = 256
RNN_GATE_WINDOW, RNN_GATE_WINDOW_STARTS = _gate_windows()


def _windowed_block_diag(w):
    n, bs = RNN_BLOCKS_PER_SUPER, RNN_BLOCK
    w = w.reshape(RNN_SUPER, n, bs, bs)
    eye = jnp.eye(n, dtype=w.dtype)
    dense = jnp.einsum('sgij,gh->sgihj', w, eye).reshape(RNN_SUPER, n * bs, n * bs)
    n_tiles = len(RNN_GATE_WINDOW_STARTS)
    dense = jnp.pad(dense, ((0, 0), (0, 0), (0, n_tiles * RNN_GATE_TILE - RNN_SUPER_W)))
    tiles = [dense[:, ks:ks + RNN_GATE_WINDOW, t * RNN_GATE_TILE:(t + 1) * RNN_GATE_TILE]
             for t, ks in enumerate(RNN_GATE_WINDOW_STARTS)]
    return jnp.stack(tiles, axis=1)


def rglru_core(proj_out, conv_w, conv_b, wa, wx, ga_b, gx_b, lam, batch, seq, *, tt=512):
    m = proj_out.shape[0]
    sw = RNN_SUPER_W
    tt = min(tt, seq)
    steps = seq // tt
    vec = lambda v: v.reshape(1, D_RNN)
    vec_spec = pl.BlockSpec((1, sw), lambda b, s, t: (0, s))
    w_spec = pl.BlockSpec((1,) + wa.shape[1:], lambda b, s, t: (s, 0, 0, 0))
    return pl.pallas_call(
        _rglru_kernel,
        grid=(batch, RNN_SUPER, steps),
        in_specs=[
            pl.BlockSpec((tt, sw), lambda b, s, t: (b * steps + t, s)),
            pl.BlockSpec((tt, sw), lambda b, s, t: (b * steps + t, RNN_SUPER + s)),
            pl.BlockSpec((CONV_WIDTH, sw), lambda b, s, t: (0, s)),
            vec_spec,
            w_spec, w_spec,
            vec_spec, vec_spec, vec_spec,
        ],
        out_specs=pl.BlockSpec((tt, sw), lambda b, s, t: (b * steps + t, s)),
        out_shape=jax.ShapeDtypeStruct((m, D_RNN), BF16),
        scratch_shapes=[
            pltpu.VMEM((tt + 8, sw), F32),
            pltpu.VMEM((tt, sw), F32),
            pltpu.VMEM((tt, sw), F32),
            pltpu.VMEM((tt, sw), F32),
            pltpu.VMEM((8, sw), F32),
        ],
        compiler_params=_params(("parallel", "parallel", "arbitrary")),
        name="rglru_core",
    )(proj_out, proj_out, conv_w, vec(conv_b), wa, wx, vec(ga_b), vec(gx_b), vec(lam))


def attention_mixer(h, hb, w_in, gn_g, gn_b, w_out, ln_g, ln_b, batch, seq):
    wb = w_in.astype(BF16)
    cos_tab, sin_tab = _rotary_tables(seq)
    qk = proj_rotary(hb, wb[:, :2 * D_RET], cos_tab, sin_tab, seq)
    rg = proj(hb, wb[:, 3 * D_RET:4 * D_RET], F32)
    vm = proj(hb, jnp.concatenate([wb[:, 2 * D_RET:3 * D_RET], wb[:, 4 * D_RET:]], axis=1), BF16,
              scaled_cols=(D_RET, D_RET + D_MOBA), scale=MOBA_HEAD_DIM ** -0.5)
    ro = retention_gn_gate(qk, vm, rg, gn_g, gn_b, batch, seq)
    mo = moba(vm, batch, seq)
    wo = w_out.astype(BF16)
    return outproj_ln(h, [ro, mo], [wo[:D_RET], wo[D_RET:]], ln_g, ln_b)


def rglru_mixer(h, hb, w_in, conv_w, conv_b, ga_w, ga_b, gx_w, gx_b, lam, w_out, ln_g, ln_b, batch, seq):
    pr = proj(hb, w_in.astype(BF16), F32, tm=512, tn=D_RNN, weights_outer=True)
    y = rglru_core(pr, conv_w, conv_b, _windowed_block_diag(ga_w).astype(BF16),
                   _windowed_block_diag(gx_w).astype(BF16), ga_b, gx_b, lam, batch, seq)
    return outproj_ln(h, [y], [w_out.astype(BF16)], ln_g, ln_b)


def kernel(x, ln_g, ln_b, ffn_w_gate, ffn_w_up, ffn_w_down, attn_w_in, ret_gn_g, ret_gn_b, attn_w_out, rnn_w_in, rnn_conv_w, rnn_conv_b, rnn_gate_a_w, rnn_gate_a_b, rnn_gate_x_w, rnn_gate_x_b, rnn_lambda, rnn_w_out):
    batch, seq, d = x.shape
    h = x.reshape(batch * seq, d)
    for layer in range(DEPTH):
        def ffn(hh, half, ln_i, emit_bf16):
            act = swiglu_up(hh, _gate_up_tiles(ffn_w_gate[layer, half], ffn_w_up[layer, half], FFN_UP_TILE))
            w_down = (0.5 * ffn_w_down[layer, half]).astype(BF16)
            return outproj_ln(hh, [act], [w_down], ln_g[layer, ln_i], ln_b[layer, ln_i],
                              tm=FFN_DOWN_ROWS, emit_bf16=emit_bf16)
        h, hb = ffn(h, 0, 0, True)
        j = layer // 2
        if layer % 2 == 0:
            h = attention_mixer(h, hb, attn_w_in[j], ret_gn_g[j], ret_gn_b[j], attn_w_out[j],
                                ln_g[layer, 1], ln_b[layer, 1], batch, seq)
        else:
            h = rglru_mixer(h, hb, rnn_w_in[j], rnn_conv_w[j], rnn_conv_b[j], rnn_gate_a_w[j], rnn_gate_a_b[j],
                            rnn_gate_x_w[j], rnn_gate_x_b[j], rnn_lambda[j], rnn_w_out[j],
                            ln_g[layer, 1], ln_b[layer, 1], batch, seq)
        h = ffn(h, 1, 2, False)
    return h.reshape(batch, seq, d)
```

```python
import functools
import math

import jax
import jax.numpy as jnp
from jax import lax
from jax.experimental import pallas as pl
from jax.experimental.pallas import tpu as pltpu

F32 = jnp.float32
BF16 = jnp.bfloat16

D_MODEL = 2048
DEPTH = 2
N_RET_HEADS = 8
RET_HEAD_DIM = 128
D_RET = N_RET_HEADS * RET_HEAD_DIM
RET_KERNEL_CHUNK = 256
ROPE_BASE = 10000.0
N_MOBA_HEADS = 8
MOBA_HEAD_DIM = 128
D_MOBA = N_MOBA_HEADS * MOBA_HEAD_DIM
MOBA_BLOCK = 256
MOBA_TOPK = 3
D_RNN = 2816
N_RNN_BLOCKS = 16
RNN_BLOCK = D_RNN // N_RNN_BLOCKS
CONV_WIDTH = 4
LRU_C = 8.0
D_FF = 5632
LN_EPS = 1e-5
DEEPNORM_ALPHA = (2.0 * DEPTH) ** 0.25

RNN_SUPER = 2
RNN_SUPER_W = D_RNN // RNN_SUPER
RNN_BLOCKS_PER_SUPER = N_RNN_BLOCKS // RNN_SUPER

NEG_BIG = -1e30

VMEM_LIMIT = 56 * 1024 * 1024

NT_DIMS = (((1,), (1,)), ((), ()))
TN_DIMS = (((0,), (0,)), ((), ()))


def _params(semantics):
    return pltpu.CompilerParams(dimension_semantics=semantics, vmem_limit_bytes=VMEM_LIMIT)


def _layer_norm(y, g, b):
    mu = jnp.mean(y, axis=-1, keepdims=True)
    yc = y - mu
    var = jnp.mean(yc * yc, axis=-1, keepdims=True)
    return yc * lax.rsqrt(var + LN_EPS) * g + b


def _proj_kernel(scaled_tiles, scale, col_axis, x_ref, w_ref, o_ref):
    acc = jnp.dot(x_ref[...], w_ref[...], preferred_element_type=F32)
    if scaled_tiles is not None:
        j = pl.program_id(col_axis)
        lo, hi = scaled_tiles
        acc = acc * jnp.where(jnp.logical_and(j >= lo, j < hi), scale, 1.0).astype(F32)
    o_ref[...] = acc.astype(o_ref.dtype)


def _proj_rotary_kernel(x_ref, w_ref, cos_ref, sin_ref, o_ref):
    acc = jnp.dot(x_ref[...], w_ref[...], preferred_element_type=F32)
    cos = cos_ref[0]
    sin = sin_ref[0]
    hd = cos.shape[-1]
    for c in range(acc.shape[1] // hd):
        t = acc[:, c * hd:(c + 1) * hd]
        rot = pltpu.roll(t, hd // 2, 1)
        o_ref[:, c * hd:(c + 1) * hd] = (t * cos + rot * sin).astype(o_ref.dtype)


def proj(x, w, out_dtype, *, tm=1024, tn=1024, weights_outer=False, scaled_cols=None, scale=1.0):
    m, k = x.shape
    n = w.shape[1]
    scaled_tiles = None if scaled_cols is None else (scaled_cols[0] // tn, scaled_cols[1] // tn)
    if weights_outer:
        grid = (n // tn, m // tm)
        row = lambda j, i: (i, 0)
        col = lambda j, i: (0, j)
        out = lambda j, i: (i, j)
    else:
        grid = (m // tm, n // tn)
        row = lambda i, j: (i, 0)
        col = lambda i, j: (0, j)
        out = lambda i, j: (i, j)
    return pl.pallas_call(
        functools.partial(_proj_kernel, scaled_tiles, scale, 0 if weights_outer else 1),
        grid=grid,
        in_specs=[pl.BlockSpec((tm, k), row), pl.BlockSpec((k, tn), col)],
        out_specs=pl.BlockSpec((tm, tn), out),
        out_shape=jax.ShapeDtypeStruct((m, n), out_dtype),
        compiler_params=_params(("parallel", "arbitrary")),
        name="proj",
    )(x, w)


def proj_rotary(x, w, cos_tab, sin_tab, seq, *, tm=1024, tn=1024):
    m, k = x.shape
    n = w.shape[1]
    hd = cos_tab.shape[-1]
    tm = min(tm, seq)
    per_tab = (n // 2) // tn
    tiles_per_seq = seq // tm
    tab_spec = pl.BlockSpec((1, tm, hd), lambda i, j: (j // per_tab, i % tiles_per_seq, 0))
    return pl.pallas_call(
        _proj_rotary_kernel,
        grid=(m // tm, n // tn),
        in_specs=[
            pl.BlockSpec((tm, k), lambda i, j: (i, 0)),
            pl.BlockSpec((k, tn), lambda i, j: (0, j)),
            tab_spec,
            tab_spec,
        ],
        out_specs=pl.BlockSpec((tm, tn), lambda i, j: (i, j)),
        out_shape=jax.ShapeDtypeStruct((m, n), BF16),
        compiler_params=_params(("parallel", "arbitrary")),
        name="proj_rotary",
    )(x, w, cos_tab, sin_tab)


def _rotary_tables(seq):
    half = RET_HEAD_DIM // 2
    inv_freq = ROPE_BASE ** (-jnp.arange(half, dtype=F32) / half)
    ang = jnp.arange(seq, dtype=F32)[:, None] * inv_freq[None, :]
    cos, sin = jnp.cos(ang), jnp.sin(ang)
    cos2 = jnp.concatenate([cos, cos], axis=-1)
    sin2 = jnp.concatenate([-sin, sin], axis=-1)
    kscale = RET_HEAD_DIM ** -0.5
    return jnp.stack([cos2, cos2 * kscale]), jnp.stack([sin2, sin2 * kscale])


def _retention_kernel(q_ref, k_ref, v_ref, g_ref, intra_ref, kdec_ref, qdec_ref, cdec_ref,
                      gng_ref, gnb_ref, o_ref, state_ref):
    @pl.when(pl.program_id(2) == 0)
    def _():
        state_ref[...] = jnp.zeros_like(state_ref)

    intra = intra_ref[0]
    kdec = kdec_ref[0]
    qdec = qdec_ref[0]
    cdec = cdec_ref[0]
    gng = gng_ref[...]
    gnb = gnb_ref[...]
    c_len = intra.shape[0]
    state = state_ref[...]
    for c in range(q_ref.shape[0] // c_len):
        rows = slice(c * c_len, (c + 1) * c_len)
        q = q_ref[rows, :]
        k = k_ref[rows, :]
        v = v_ref[rows, :]
        scores = lax.dot_general(q, k, NT_DIMS, preferred_element_type=F32) * intra
        inner = jnp.dot(scores.astype(BF16), v, preferred_element_type=F32)
        cross = jnp.dot(q, state.astype(BF16), preferred_element_type=F32) * qdec
        out = inner + cross
        k_decayed = (k.astype(F32) * kdec).astype(BF16)
        kv = lax.dot_general(k_decayed, v, TN_DIMS, preferred_element_type=F32)
        state = state * cdec + kv
        mu = jnp.mean(out, axis=-1, keepdims=True)
        oc = out - mu
        var = jnp.mean(oc * oc, axis=-1, keepdims=True)
        normed = oc * lax.rsqrt(var + LN_EPS) * gng + gnb
        gate = g_ref[rows, :]
        o_ref[rows, :] = (gate * jax.nn.sigmoid(gate) * normed).astype(o_ref.dtype)
    state_ref[...] = state


def _retention_tables(c_len):
    h = N_RET_HEADS
    hd = RET_HEAD_DIM
    log_g = jnp.log1p(-jnp.exp2(-5.0 - jnp.arange(h, dtype=F32)))
    c = jnp.arange(c_len, dtype=F32)
    rel = c[:, None] - c[None, :]
    intra = jnp.where(rel >= 0, jnp.exp(jnp.maximum(rel, 0.0)[None] * log_g[:, None, None]), 0.0)
    k_dec = jnp.exp((c_len - 1 - c)[None, :] * log_g[:, None])
    q_dec = jnp.exp((c + 1)[None, :] * log_g[:, None])
    chunk_dec = jnp.exp(c_len * log_g)
    return (intra.astype(F32),
            jnp.broadcast_to(k_dec[:, :, None], (h, c_len, hd)),
            jnp.broadcast_to(q_dec[:, :, None], (h, c_len, hd)),
            jnp.broadcast_to(chunk_dec[:, None, None], (h, hd, hd)))


def retention_gn_gate(qk, vm, rg, gn_g, gn_b, batch, seq, *, rows=1024):
    m = qk.shape[0]
    h = N_RET_HEADS
    hd = RET_HEAD_DIM
    rows = min(rows, seq)
    steps = seq // rows
    tables = _retention_tables(min(RET_KERNEL_CHUNK, rows))
    intra, kdec, qdec, cdec = tables
    row_map = lambda off: (lambda b, hh, t: (b * steps + t, off + hh))
    tab_specs = [pl.BlockSpec((1,) + tab.shape[1:], lambda b, hh, t: (hh, 0, 0)) for tab in tables]
    vec_spec = pl.BlockSpec((1, hd), lambda b, hh, t: (0, hh))
    return pl.pallas_call(
        _retention_kernel,
        grid=(batch, h, steps),
        in_specs=[
            pl.BlockSpec((rows, hd), row_map(0)),
            pl.BlockSpec((rows, hd), row_map(h)),
            pl.BlockSpec((rows, hd), row_map(0)),
            pl.BlockSpec((rows, hd), row_map(0)),
            *tab_specs,
            vec_spec, vec_spec,
        ],
        out_specs=pl.BlockSpec((rows, hd), row_map(0)),
        out_shape=jax.ShapeDtypeStruct((m, h * hd), BF16),
        scratch_shapes=[pltpu.VMEM((hd, hd), F32)],
        compiler_params=_params(("parallel", "parallel", "arbitrary")),
        name="retention",
    )(qk, qk, vm, rg, intra, kdec, qdec, cdec, gn_g.reshape(1, h * hd), gn_b.reshape(1, h * hd))


def _moba_kernel(blk, q_ref, k_ref, v_ref, o_ref, kaug_ref, vt_ref, kmean_ref, qaug_ref, s_ref, smax_ref,
                 m_ref, l_ref, acc_ref):
    i = pl.program_id(2)
    nb, hd = kmean_ref.shape
    n_tiles, _, tile = vt_ref.shape
    per_tile = tile // blk

    @pl.when(i == 0)
    def _():
        lane = lax.broadcasted_iota(jnp.int32, (blk, hd), 1)
        for j in range(nb):
            rows = slice(j * blk, (j + 1) * blk)
            kb = k_ref[rows, :]
            kmean_ref[j:j + 1, :] = jnp.mean(kb.astype(F32), axis=0, keepdims=True)
            kaug_ref[rows, 0:hd] = kb
            kaug_ref[rows, hd:2 * hd] = jnp.where(lane == j, 1.0, 0.0).astype(BF16)
        for g in range(n_tiles):
            vt_ref[g] = v_ref[g * tile:(g + 1) * tile, :].T

    q = q_ref[...]

    gate = lax.dot_general(kmean_ref[...].astype(BF16), q, NT_DIMS, preferred_element_type=F32)
    n_idx = lax.broadcasted_iota(jnp.int32, gate.shape, 0)
    q_blk = i * per_tile + lax.broadcasted_iota(jnp.int32, gate.shape, 1) // blk
    gate = jnp.where(n_idx < q_blk, gate, -jnp.inf)
    n_f = n_idx.astype(F32)
    bias = jnp.where(n_idx == q_blk, 0.0, NEG_BIG).astype(F32)
    for _ in range(min(MOBA_TOPK, nb)):
        best = jnp.max(gate, axis=0, keepdims=True)
        first = jnp.min(jnp.where(gate == best, n_f, float(nb)), axis=0, keepdims=True)
        hit = jnp.logical_and(n_f == first, best > -jnp.inf)
        bias = jnp.where(hit, 0.0, bias)
        gate = jnp.where(hit, -jnp.inf, gate)
    bias_rows = jnp.concatenate([bias, jnp.zeros((hd - nb, tile), F32)], axis=0)
    qaug_ref[:, 0:hd] = q
    qaug_ref[:, hd:2 * hd] = bias_rows.T.astype(BF16)

    def produce(slot, g):
        start = pl.multiple_of(g * tile, tile)
        s = lax.dot_general(kaug_ref[pl.ds(start, tile), :], qaug_ref[...], NT_DIMS,
                            preferred_element_type=F32)
        s_ref[slot] = s
        smax_ref[slot] = jnp.max(s, axis=0, keepdims=True)

    m_ref[...] = jnp.full(m_ref.shape, NEG_BIG, F32)
    l_ref[...] = jnp.zeros_like(l_ref)
    acc_ref[...] = jnp.zeros_like(acc_ref)

    def consume(g, s_g, s_max):
        m_run = m_ref[...]
        m_new = jnp.maximum(m_run, s_max)
        corr = jnp.exp(m_run - m_new)
        p_g = jnp.exp(s_g - m_new)
        m_ref[...] = m_new
        l_ref[...] = l_ref[...] * corr + jnp.sum(p_g, axis=0, keepdims=True)
        acc_ref[...] = acc_ref[...] * corr + jnp.dot(vt_ref[g], p_g.astype(BF16), preferred_element_type=F32)

    def consume_own(slot):
        s = s_ref[slot]
        kpos = lax.broadcasted_iota(jnp.int32, s.shape, 0)
        qpos = lax.broadcasted_iota(jnp.int32, s.shape, 1)
        s = jnp.where(kpos <= qpos, s, NEG_BIG)
        consume(i, s, jnp.max(s, axis=0, keepdims=True))

    produce(0, 0)

    def two_tiles(t, carry):
        g = 2 * t
        consume(g, s_ref[0], smax_ref[0])
        produce(1, g + 1)
        consume(g + 1, s_ref[1], smax_ref[1])
        produce(0, g + 2)
        return carry

    lax.fori_loop(0, i // 2, two_tiles, 0)

    @pl.when(i % 2 == 1)
    def _():
        consume(i - 1, s_ref[0], smax_ref[0])
        produce(1, i)
        consume_own(1)

    @pl.when(i % 2 == 0)
    def _():
        consume_own(0)

    o_ref[...] = (acc_ref[...] / l_ref[...]).T.astype(o_ref.dtype)


def moba(vm, batch, seq, *, blocks_per_tile=4):
    m = vm.shape[0]
    h = N_MOBA_HEADS
    hd = MOBA_HEAD_DIM
    blk = MOBA_BLOCK
    nb = seq // blk
    per_tile = min(blocks_per_tile, nb)
    tile = per_tile * blk
    n_tiles = nb // per_tile
    assert nb % per_tile == 0 and nb <= hd
    return pl.pallas_call(
        functools.partial(_moba_kernel, blk),
        grid=(batch, h, n_tiles),
        in_specs=[
            pl.BlockSpec((tile, hd), lambda b, hh, i: (b * n_tiles + i, h + hh)),
            pl.BlockSpec((seq, hd), lambda b, hh, i: (b, 2 * h + hh)),
            pl.BlockSpec((seq, hd), lambda b, hh, i: (b, 3 * h + hh)),
        ],
        out_specs=pl.BlockSpec((tile, hd), lambda b, hh, i: (b * n_tiles + i, hh)),
        out_shape=jax.ShapeDtypeStruct((m, h * hd), BF16),
        scratch_shapes=[
            pltpu.VMEM((seq, 2 * hd), BF16),
            pltpu.VMEM((n_tiles, hd, tile), BF16),
            pltpu.VMEM((nb, hd), F32),
            pltpu.VMEM((tile, 2 * hd), BF16),
            pltpu.VMEM((2, tile, tile), F32),
            pltpu.VMEM((2, 1, tile), F32),
            pltpu.VMEM((1, tile), F32),
            pltpu.VMEM((1, tile), F32),
            pltpu.VMEM((hd, tile), F32),
        ],
        compiler_params=_params(("parallel", "parallel", "arbitrary")),
        name="moba",
    )(vm, vm, vm)


def _outproj_ln_kernel(n_in, *refs):
    h_ref = refs[0]
    a_refs = refs[1:1 + n_in]
    w_refs = refs[1 + n_in:1 + 2 * n_in]
    g_ref, b_ref, *o_refs = refs[1 + 2 * n_in:]
    for r in range(0, h_ref.shape[0], OUTPROJ_SUB_ROWS):
        rows = slice(r, r + OUTPROJ_SUB_ROWS)
        acc = jnp.dot(a_refs[0][rows, :], w_refs[0][...], preferred_element_type=F32)
        for a_ref, w_ref in zip(a_refs[1:], w_refs[1:]):
            acc += jnp.dot(a_ref[rows, :], w_ref[...], preferred_element_type=F32)
        y = DEEPNORM_ALPHA * h_ref[rows, :] + acc
        out = _layer_norm(y, g_ref[...], b_ref[...])
        for o_ref in o_refs:
            o_ref[rows, :] = out.astype(o_ref.dtype)


OUTPROJ_SUB_ROWS = 256


def outproj_ln(h, acts, weights, ln_g, ln_b, *, tm=512, emit_bf16=False):
    m, d = h.shape
    n_in = len(acts)
    row_spec = pl.BlockSpec((tm, d), lambda i: (i, 0))
    in_specs = [row_spec]
    in_specs += [pl.BlockSpec((tm, a.shape[1]), lambda i: (i, 0)) for a in acts]
    in_specs += [pl.BlockSpec(w.shape, lambda i: (0, 0)) for w in weights]
    in_specs += [pl.BlockSpec((1, d), lambda i: (0, 0))] * 2
    out_dtypes = (F32, BF16) if emit_bf16 else (F32,)
    outs = pl.pallas_call(
        functools.partial(_outproj_ln_kernel, n_in),
        grid=(m // tm,),
        in_specs=in_specs,
        out_specs=[row_spec] * len(out_dtypes),
        out_shape=[jax.ShapeDtypeStruct((m, d), dt) for dt in out_dtypes],
        compiler_params=_params(("parallel",)),
        name="outproj_ln",
    )(h, *acts, *weights, ln_g.reshape(1, d), ln_b.reshape(1, d))
    return outs if emit_bf16 else outs[0]


def _swiglu_up_kernel(x_ref, wg_ref, wu_ref, o_ref):
    xb = x_ref[...].astype(BF16)
    gate = jnp.dot(xb, wg_ref[...], preferred_element_type=F32)
    up = jnp.dot(xb, wu_ref[...], preferred_element_type=F32)
    o_ref[...] = (gate * jax.nn.sigmoid(gate) * up).astype(o_ref.dtype)


def swiglu_up(x, w_gate, w_up, *, tm=1024, tf=512):
    m, d = x.shape
    f = w_gate.shape[1]
    w_spec = pl.BlockSpec((d, tf), lambda j, i: (0, j))
    return pl.pallas_call(
        _swiglu_up_kernel,
        grid=(f // tf, m // tm),
        in_specs=[pl.BlockSpec((tm, d), lambda j, i: (i, 0)), w_spec, w_spec],
        out_specs=pl.BlockSpec((tm, tf), lambda j, i: (i, j)),
        out_shape=jax.ShapeDtypeStruct((m, f), BF16),
        compiler_params=_params(("parallel", "arbitrary")),
        name="swiglu_up",
    )(x, w_gate, w_up)


FFN_DOWN_ROWS = 256


def _gelu_tanh(x):
    return 0.5 * x * (1.0 + jnp.tanh(math.sqrt(2.0 / math.pi) * (x + 0.044715 * (x * x * x))))


def _softplus(x):
    return jnp.maximum(x, 0.0) + jnp.log1p(jnp.exp(-jnp.abs(x)))


def _rglru_kernel(gate_ref, xr_ref, cw_ref, cb_ref, wa_ref, wx_ref, gab_ref, gxb_ref, lam_ref, o_ref,
                  xext_ref, a_ref, b_ref, hs_ref, carry_ref):
    tt = xr_ref.shape[0]
    halo = 8

    @pl.when(pl.program_id(2) == 0)
    def _():
        xext_ref[0:halo, :] = jnp.zeros((halo, xext_ref.shape[1]), F32)
        carry_ref[...] = jnp.zeros_like(carry_ref)

    xext_ref[halo:halo + tt, :] = xr_ref[...]
    cw = cw_ref[...]
    u = cb_ref[...]
    for tap in range(CONV_WIDTH):
        off = halo - (CONV_WIDTH - 1) + tap
        u = u + cw[tap:tap + 1, :] * xext_ref[off:off + tt, :]
    xext_ref[0:halo, :] = xext_ref[tt:tt + halo, :]

    ub = u.astype(BF16)
    sw = ub.shape[1]

    def gate_pre(w_ref):
        parts = [jnp.dot(ub[:, ks:ks + RNN_GATE_WINDOW], w_ref[0, n], preferred_element_type=F32)
                 for n, ks in enumerate(RNN_GATE_WINDOW_STARTS)]
        return jnp.concatenate(parts, axis=1)[:, :sw]

    r = jax.nn.sigmoid(gate_pre(wa_ref) + gab_ref[...])
    gi = jax.nn.sigmoid(gate_pre(wx_ref) + gxb_ref[...])
    log_a = -LRU_C * r * _softplus(-lam_ref[...])
    a = jnp.exp(log_a)
    a_ref[...] = a
    b_ref[...] = jnp.sqrt(1.0 - a * a) * (gi * u)

    def step(t, h):
        h = a_ref[pl.ds(t, 1), :] * h + b_ref[pl.ds(t, 1), :]
        hs_ref[pl.ds(t, 1), :] = h
        return h

    h_last = lax.fori_loop(0, tt, step, carry_ref[0:1, :], unroll=8)
    carry_ref[0:1, :] = h_last
    o_ref[...] = (hs_ref[...] * _gelu_tanh(gate_ref[...])).astype(o_ref.dtype)


def _gate_windows():
    spans = []
    for lo in range(0, RNN_SUPER_W, RNN_GATE_TILE):
        hi = min(lo + RNN_GATE_TILE, RNN_SUPER_W)
        first, last = lo // RNN_BLOCK, (hi - 1) // RNN_BLOCK
        spans.append((first * RNN_BLOCK // 128 * 128, -(-(last + 1) * RNN_BLOCK // 128) * 128))
    width = max(e - s for s, e in spans)
    return width, tuple(min(s, RNN_SUPER_W - width) for s, _ in spans)


RNN_GATE_TILE = 256
RNN_GATE_WINDOW, RNN_GATE_WINDOW_STARTS = _gate_windows()


def _windowed_block_diag(w):
    n, bs = RNN_BLOCKS_PER_SUPER, RNN_BLOCK
    w = w.reshape(RNN_SUPER, n, bs, bs)
    eye = jnp.eye(n, dtype=w.dtype)
    dense = jnp.einsum('sgij,gh->sgihj', w, eye).reshape(RNN_SUPER, n * bs, n * bs)
    n_tiles = len(RNN_GATE_WINDOW_STARTS)
    dense = jnp.pad(dense, ((0, 0), (0, 0), (0, n_tiles * RNN_GATE_TILE - RNN_SUPER_W)))
    tiles = [dense[:, ks:ks + RNN_GATE_WINDOW, t * RNN_GATE_TILE:(t + 1) * RNN_GATE_TILE]
             for t, ks in enumerate(RNN_GATE_WINDOW_STARTS)]
    return jnp.stack(tiles, axis=1)


def rglru_core(proj_out, conv_w, conv_b, wa, wx, ga_b, gx_b, lam, batch, seq, *, tt=512):
    m = proj_out.shape[0]
    sw = RNN_SUPER_W
    tt = min(tt, seq)
    steps = seq // tt
    vec = lambda v: v.reshape(1, D_RNN)
    vec_spec = pl.BlockSpec((1, sw), lambda b, s, t: (0, s))
    w_spec = pl.BlockSpec((1,) + wa.shape[1:], lambda b, s, t: (s, 0, 0, 0))
    return pl.pallas_call(
        _rglru_kernel,
        grid=(batch, RNN_SUPER, steps),
        in_specs=[
            pl.BlockSpec((tt, sw), lambda b, s, t: (b * steps + t, s)),
            pl.BlockSpec((tt, sw), lambda b, s, t: (b * steps + t, RNN_SUPER + s)),
            pl.BlockSpec((CONV_WIDTH, sw), lambda b, s, t: (0, s)),
            vec_spec,
            w_spec, w_spec,
            vec_spec, vec_spec, vec_spec,
        ],
        out_specs=pl.BlockSpec((tt, sw), lambda b, s, t: (b * steps + t, s)),
        out_shape=jax.ShapeDtypeStruct((m, D_RNN), BF16),
        scratch_shapes=[
            pltpu.VMEM((tt + 8, sw), F32),
            pltpu.VMEM((tt, sw), F32),
            pltpu.VMEM((tt, sw), F32),
            pltpu.VMEM((tt, sw), F32),
            pltpu.VMEM((8, sw), F32),
        ],
        compiler_params=_params(("parallel", "parallel", "arbitrary")),
        name="rglru_core",
    )(proj_out, proj_out, conv_w, vec(conv_b), wa, wx, vec(ga_b), vec(gx_b), vec(lam))


def attention_mixer(h, hb, w_in, gn_g, gn_b, w_out, ln_g, ln_b, batch, seq):
    wb = w_in.astype(BF16)
    cos_tab, sin_tab = _rotary_tables(seq)
    qk = proj_rotary(hb, wb[:, :2 * D_RET], cos_tab, sin_tab, seq)
    rg = proj(hb, wb[:, 3 * D_RET:4 * D_RET], F32)
    vm = proj(hb, jnp.concatenate([wb[:, 2 * D_RET:3 * D_RET], wb[:, 4 * D_RET:]], axis=1), BF16,
              scaled_cols=(D_RET, D_RET + D_MOBA), scale=MOBA_HEAD_DIM ** -0.5)
    ro = retention_gn_gate(qk, vm, rg, gn_g, gn_b, batch, seq)
    mo = moba(vm, batch, seq)
    wo = w_out.astype(BF16)
    return outproj_ln(h, [ro, mo], [wo[:D_RET], wo[D_RET:]], ln_g, ln_b)


def rglru_mixer(h, hb, w_in, conv_w, conv_b, ga_w, ga_b, gx_w, gx_b, lam, w_out, ln_g, ln_b, batch, seq):
    pr = proj(hb, w_in.astype(BF16), F32, tm=512, tn=D_RNN, weights_outer=True)
    y = rglru_core(pr, conv_w, conv_b, _windowed_block_diag(ga_w).astype(BF16),
                   _windowed_block_diag(gx_w).astype(BF16), ga_b, gx_b, lam, batch, seq)
    return outproj_ln(h, [y], [w_out.astype(BF16)], ln_g, ln_b)


def kernel(x, ln_g, ln_b, ffn_w_gate, ffn_w_up, ffn_w_down, attn_w_in, ret_gn_g, ret_gn_b, attn_w_out, rnn_w_in, rnn_conv_w, rnn_conv_b, rnn_gate_a_w, rnn_gate_a_b, rnn_gate_x_w, rnn_gate_x_b, rnn_lambda, rnn_w_out):
    batch, seq, d = x.shape
    h = x.reshape(batch * seq, d)
    for layer in range(DEPTH):
        def ffn(hh, half, ln_i, emit_bf16):
            act = swiglu_up(hh, ffn_w_gate[layer, half].astype(BF16), ffn_w_up[layer, half].astype(BF16))
            w_down = (0.5 * ffn_w_down[layer, half]).astype(BF16)
            return outproj_ln(hh, [act], [w_down], ln_g[layer, ln_i], ln_b[layer, ln_i],
                              tm=FFN_DOWN_ROWS, emit_bf16=emit_bf16)
        h, hb = ffn(h, 0, 0, True)
        j = layer // 2
        if layer % 2 == 0:
            h = attention_mixer(h, hb, attn_w_in[j], ret_gn_g[j], ret_gn_b[j], attn_w_out[j],
                                ln_g[layer, 1], ln_b[layer, 1], batch, seq)
        else:
            h = rglru_mixer(h, hb, rnn_w_in[j], rnn_conv_w[j], rnn_conv_b[j], rnn_gate_a_w[j], rnn_gate_a_b[j],
                            rnn_gate_x_w[j], rnn_gate_x_b[j], rnn_lambda[j], rnn_w_out[j],
                            ln_g[layer, 1], ln_b[layer, 1], batch, seq)
        h = ffn(h, 1, 2, False)
    return h.reshape(batch, seq, d)
```

```python
import functools
import math

import jax
import jax.numpy as jnp
from jax import lax
from jax.experimental import pallas as pl
from jax.experimental.pallas import tpu as pltpu

F32 = jnp.float32
BF16 = jnp.bfloat16

D_MODEL = 2048
DEPTH = 2
N_RET_HEADS = 8
RET_HEAD_DIM = 128
D_RET = N_RET_HEADS * RET_HEAD_DIM
RET_KERNEL_CHUNK = 256
ROPE_BASE = 10000.0
N_MOBA_HEADS = 8
MOBA_HEAD_DIM = 128
D_MOBA = N_MOBA_HEADS * MOBA_HEAD_DIM
MOBA_BLOCK = 256
MOBA_TOPK = 3
D_RNN = 2816
N_RNN_BLOCKS = 16
RNN_BLOCK = D_RNN // N_RNN_BLOCKS
CONV_WIDTH = 4
LRU_C = 8.0
D_FF = 5632
LN_EPS = 1e-5
DEEPNORM_ALPHA = (2.0 * DEPTH) ** 0.25

RNN_SUPER = 2
RNN_SUPER_W = D_RNN // RNN_SUPER
RNN_BLOCKS_PER_SUPER = N_RNN_BLOCKS // RNN_SUPER

NEG_BIG = -1e30

VMEM_LIMIT = 56 * 1024 * 1024

NT_DIMS = (((1,), (1,)), ((), ()))
TN_DIMS = (((0,), (0,)), ((), ()))


def _params(semantics):
    return pltpu.CompilerParams(dimension_semantics=semantics, vmem_limit_bytes=VMEM_LIMIT)


def _layer_norm(y, g, b):
    mu = jnp.mean(y, axis=-1, keepdims=True)
    yc = y - mu
    var = jnp.mean(yc * yc, axis=-1, keepdims=True)
    return yc * lax.rsqrt(var + LN_EPS) * g + b


def _proj_kernel(scaled_tiles, scale, col_axis, x_ref, w_ref, o_ref):
    acc = jnp.dot(x_ref[...], w_ref[...], preferred_element_type=F32)
    if scaled_tiles is not None:
        j = pl.program_id(col_axis)
        lo, hi = scaled_tiles
        acc = acc * jnp.where(jnp.logical_and(j >= lo, j < hi), scale, 1.0).astype(F32)
    o_ref[...] = acc.astype(o_ref.dtype)


def _proj_rotary_kernel(x_ref, w_ref, cos_ref, sin_ref, o_ref):
    acc = jnp.dot(x_ref[...], w_ref[...], preferred_element_type=F32)
    cos = cos_ref[0]
    sin = sin_ref[0]
    hd = cos.shape[-1]
    for c in range(acc.shape[1] // hd):
        t = acc[:, c * hd:(c + 1) * hd]
        rot = pltpu.roll(t, hd // 2, 1)
        o_ref[:, c * hd:(c + 1) * hd] = (t * cos + rot * sin).astype(o_ref.dtype)


def proj(x, w, out_dtype, *, tm=1024, tn=1024, weights_outer=False, scaled_cols=None, scale=1.0):
    m, k = x.shape
    n = w.shape[1]
    scaled_tiles = None if scaled_cols is None else (scaled_cols[0] // tn, scaled_cols[1] // tn)
    if weights_outer:
        grid = (n // tn, m // tm)
        row = lambda j, i: (i, 0)
        col = lambda j, i: (0, j)
        out = lambda j, i: (i, j)
    else:
        grid = (m // tm, n // tn)
        row = lambda i, j: (i, 0)
        col = lambda i, j: (0, j)
        out = lambda i, j: (i, j)
    return pl.pallas_call(
        functools.partial(_proj_kernel, scaled_tiles, scale, 0 if weights_outer else 1),
        grid=grid,
        in_specs=[pl.BlockSpec((tm, k), row), pl.BlockSpec((k, tn), col)],
        out_specs=pl.BlockSpec((tm, tn), out),
        out_shape=jax.ShapeDtypeStruct((m, n), out_dtype),
        compiler_params=_params(("parallel", "arbitrary")),
        name="proj",
    )(x, w)


def proj_rotary(x, w, cos_tab, sin_tab, seq, *, tm=1024, tn=1024):
    m, k = x.shape
    n = w.shape[1]
    hd = cos_tab.shape[-1]
    tm = min(tm, seq)
    per_tab = (n // 2) // tn
    tiles_per_seq = seq // tm
    tab_spec = pl.BlockSpec((1, tm, hd), lambda i, j: (j // per_tab, i % tiles_per_seq, 0))
    return pl.pallas_call(
        _proj_rotary_kernel,
        grid=(m // tm, n // tn),
        in_specs=[
            pl.BlockSpec((tm, k), lambda i, j: (i, 0)),
            pl.BlockSpec((k, tn), lambda i, j: (0, j)),
            tab_spec,
            tab_spec,
        ],
        out_specs=pl.BlockSpec((tm, tn), lambda i, j: (i, j)),
        out_shape=jax.ShapeDtypeStruct((m, n), BF16),
        compiler_params=_params(("parallel", "arbitrary")),
        name="proj_rotary",
    )(x, w, cos_tab, sin_tab)


def _rotary_tables(seq):
    half = RET_HEAD_DIM // 2
    inv_freq = ROPE_BASE ** (-jnp.arange(half, dtype=F32) / half)
    ang = jnp.arange(seq, dtype=F32)[:, None] * inv_freq[None, :]
    cos, sin = jnp.cos(ang), jnp.sin(ang)
    cos2 = jnp.concatenate([cos, cos], axis=-1)
    sin2 = jnp.concatenate([-sin, sin], axis=-1)
    kscale = RET_HEAD_DIM ** -0.5
    return jnp.stack([cos2, cos2 * kscale]), jnp.stack([sin2, sin2 * kscale])


def _retention_kernel(q_ref, k_ref, v_ref, g_ref, intra_ref, kdec_ref, qdec_ref, cdec_ref,
                      gng_ref, gnb_ref, o_ref, state_ref):
    @pl.when(pl.program_id(2) == 0)
    def _():
        state_ref[...] = jnp.zeros_like(state_ref)

    intra = intra_ref[0]
    kdec = kdec_ref[0]
    qdec = qdec_ref[0]
    cdec = cdec_ref[0]
    gng = gng_ref[...]
    gnb = gnb_ref[...]
    c_len = intra.shape[0]
    state = state_ref[...]
    for c in range(q_ref.shape[0] // c_len):
        rows = slice(c * c_len, (c + 1) * c_len)
        q = q_ref[rows, :]
        k = k_ref[rows, :]
        v = v_ref[rows, :]
        scores = lax.dot_general(q, k, NT_DIMS, preferred_element_type=F32) * intra
        inner = jnp.dot(scores.astype(BF16), v, preferred_element_type=F32)
        cross = jnp.dot(q, state.astype(BF16), preferred_element_type=F32) * qdec
        out = inner + cross
        k_decayed = (k.astype(F32) * kdec).astype(BF16)
        kv = lax.dot_general(k_decayed, v, TN_DIMS, preferred_element_type=F32)
        state = state * cdec + kv
        mu = jnp.mean(out, axis=-1, keepdims=True)
        oc = out - mu
        var = jnp.mean(oc * oc, axis=-1, keepdims=True)
        normed = oc * lax.rsqrt(var + LN_EPS) * gng + gnb
        gate = g_ref[rows, :]
        o_ref[rows, :] = (gate * jax.nn.sigmoid(gate) * normed).astype(o_ref.dtype)
    state_ref[...] = state


def _retention_tables(c_len):
    h = N_RET_HEADS
    hd = RET_HEAD_DIM
    log_g = jnp.log1p(-jnp.exp2(-5.0 - jnp.arange(h, dtype=F32)))
    c = jnp.arange(c_len, dtype=F32)
    rel = c[:, None] - c[None, :]
    intra = jnp.where(rel >= 0, jnp.exp(jnp.maximum(rel, 0.0)[None] * log_g[:, None, None]), 0.0)
    k_dec = jnp.exp((c_len - 1 - c)[None, :] * log_g[:, None])
    q_dec = jnp.exp((c + 1)[None, :] * log_g[:, None])
    chunk_dec = jnp.exp(c_len * log_g)
    return (intra.astype(F32),
            jnp.broadcast_to(k_dec[:, :, None], (h, c_len, hd)),
            jnp.broadcast_to(q_dec[:, :, None], (h, c_len, hd)),
            jnp.broadcast_to(chunk_dec[:, None, None], (h, hd, hd)))


def retention_gn_gate(qk, vm, rg, gn_g, gn_b, batch, seq, *, rows=1024):
    m = qk.shape[0]
    h = N_RET_HEADS
    hd = RET_HEAD_DIM
    rows = min(rows, seq)
    steps = seq // rows
    tables = _retention_tables(min(RET_KERNEL_CHUNK, rows))
    intra, kdec, qdec, cdec = tables
    row_map = lambda off: (lambda b, hh, t: (b * steps + t, off + hh))
    tab_specs = [pl.BlockSpec((1,) + tab.shape[1:], lambda b, hh, t: (hh, 0, 0)) for tab in tables]
    vec_spec = pl.BlockSpec((1, hd), lambda b, hh, t: (0, hh))
    return pl.pallas_call(
        _retention_kernel,
        grid=(batch, h, steps),
        in_specs=[
            pl.BlockSpec((rows, hd), row_map(0)),
            pl.BlockSpec((rows, hd), row_map(h)),
            pl.BlockSpec((rows, hd), row_map(0)),
            pl.BlockSpec((rows, hd), row_map(0)),
            *tab_specs,
            vec_spec, vec_spec,
        ],
        out_specs=pl.BlockSpec((rows, hd), row_map(0)),
        out_shape=jax.ShapeDtypeStruct((m, h * hd), BF16),
        scratch_shapes=[pltpu.VMEM((hd, hd), F32)],
        compiler_params=_params(("parallel", "parallel", "arbitrary")),
        name="retention",
    )(qk, qk, vm, rg, intra, kdec, qdec, cdec, gn_g.reshape(1, h * hd), gn_b.reshape(1, h * hd))


def _moba_kernel(blk, q_ref, k_ref, v_ref, o_ref, kaug_ref, vt_ref, kmean_ref, qaug_ref, s_ref, smax_ref,
                 m_ref, l_ref, acc_ref):
    i = pl.program_id(2)
    nb, hd = kmean_ref.shape
    n_tiles, _, tile = vt_ref.shape
    per_tile = tile // blk

    @pl.when(i == 0)
    def _():
        lane = lax.broadcasted_iota(jnp.int32, (blk, hd), 1)
        for j in range(nb):
            rows = slice(j * blk, (j + 1) * blk)
            kb = k_ref[rows, :]
            kmean_ref[j:j + 1, :] = jnp.mean(kb.astype(F32), axis=0, keepdims=True)
            kaug_ref[rows, 0:hd] = kb
            kaug_ref[rows, hd:2 * hd] = jnp.where(lane == j, 1.0, 0.0).astype(BF16)
        for g in range(n_tiles):
            vt_ref[g] = v_ref[g * tile:(g + 1) * tile, :].T

    q = q_ref[...]

    gate = lax.dot_general(kmean_ref[...].astype(BF16), q, NT_DIMS, preferred_element_type=F32)
    n_idx = lax.broadcasted_iota(jnp.int32, gate.shape, 0)
    q_blk = i * per_tile + lax.broadcasted_iota(jnp.int32, gate.shape, 1) // blk
    gate = jnp.where(n_idx < q_blk, gate, -jnp.inf)
    n_f = n_idx.astype(F32)
    bias = jnp.where(n_idx == q_blk, 0.0, NEG_BIG).astype(F32)
    for _ in range(min(MOBA_TOPK, nb)):
        best = jnp.max(gate, axis=0, keepdims=True)
        first = jnp.min(jnp.where(gate == best, n_f, float(nb)), axis=0, keepdims=True)
        hit = jnp.logical_and(n_f == first, best > -jnp.inf)
        bias = jnp.where(hit, 0.0, bias)
        gate = jnp.where(hit, -jnp.inf, gate)
    bias_rows = jnp.concatenate([bias, jnp.zeros((hd - nb, tile), F32)], axis=0)
    qaug_ref[:, 0:hd] = q
    qaug_ref[:, hd:2 * hd] = bias_rows.T.astype(BF16)

    def produce(slot, g):
        start = pl.multiple_of(g * tile, tile)
        s = lax.dot_general(kaug_ref[pl.ds(start, tile), :], qaug_ref[...], NT_DIMS,
                            preferred_element_type=F32)
        s_ref[slot] = s
        smax_ref[slot] = jnp.max(s, axis=0, keepdims=True)

    m_ref[...] = jnp.full(m_ref.shape, NEG_BIG, F32)
    l_ref[...] = jnp.zeros_like(l_ref)
    acc_ref[...] = jnp.zeros_like(acc_ref)

    def consume(g, s_g, s_max):
        m_run = m_ref[...]
        m_new = jnp.maximum(m_run, s_max)
        corr = jnp.exp(m_run - m_new)
        p_g = jnp.exp(s_g - m_new)
        m_ref[...] = m_new
        l_ref[...] = l_ref[...] * corr + jnp.sum(p_g, axis=0, keepdims=True)
        acc_ref[...] = acc_ref[...] * corr + jnp.dot(vt_ref[g], p_g.astype(BF16), preferred_element_type=F32)

    def consume_own(slot):
        s = s_ref[slot]
        kpos = lax.broadcasted_iota(jnp.int32, s.shape, 0)
        qpos = lax.broadcasted_iota(jnp.int32, s.shape, 1)
        s = jnp.where(kpos <= qpos, s, NEG_BIG)
        consume(i, s, jnp.max(s, axis=0, keepdims=True))

    produce(0, 0)

    def two_tiles(t, carry):
        g = 2 * t
        consume(g, s_ref[0], smax_ref[0])
        produce(1, g + 1)
        consume(g + 1, s_ref[1], smax_ref[1])
        produce(0, g + 2)
        return carry

    lax.fori_loop(0, i // 2, two_tiles, 0)

    @pl.when(i % 2 == 1)
    def _():
        consume(i - 1, s_ref[0], smax_ref[0])
        produce(1, i)
        consume_own(1)

    @pl.when(i % 2 == 0)
    def _():
        consume_own(0)

    o_ref[...] = (acc_ref[...] / l_ref[...]).T.astype(o_ref.dtype)


def moba(vm, batch, seq, *, blocks_per_tile=4):
    m = vm.shape[0]
    h = N_MOBA_HEADS
    hd = MOBA_HEAD_DIM
    blk = MOBA_BLOCK
    nb = seq // blk
    per_tile = min(blocks_per_tile, nb)
    tile = per_tile * blk
    n_tiles = nb // per_tile
    assert nb % per_tile == 0 and nb <= hd
    return pl.pallas_call(
        functools.partial(_moba_kernel, blk),
        grid=(batch, h, n_tiles),
        in_specs=[
            pl.BlockSpec((tile, hd), lambda b, hh, i: (b * n_tiles + i, h + hh)),
            pl.BlockSpec((seq, hd), lambda b, hh, i: (b, 2 * h + hh)),
            pl.BlockSpec((seq, hd), lambda b, hh, i: (b, 3 * h + hh)),
        ],
        out_specs=pl.BlockSpec((tile, hd), lambda b, hh, i: (b * n_tiles + i, hh)),
        out_shape=jax.ShapeDtypeStruct((m, h * hd), BF16),
        scratch_shapes=[
            pltpu.VMEM((seq, 2 * hd), BF16),
            pltpu.VMEM((n_tiles, hd, tile), BF16),
            pltpu.VMEM((nb, hd), F32),
            pltpu.VMEM((tile, 2 * hd), BF16),
            pltpu.VMEM((2, tile, tile), F32),
            pltpu.VMEM((2, 1, tile), F32),
            pltpu.VMEM((1, tile), F32),
            pltpu.VMEM((1, tile), F32),
            pltpu.VMEM((hd, tile), F32),
        ],
        compiler_params=_params(("parallel", "parallel", "arbitrary")),
        name="moba",
    )(vm, vm, vm)


def _outproj_ln_kernel(n_in, *refs):
    h_ref = refs[0]
    a_refs = refs[1:1 + n_in]
    w_refs = refs[1 + n_in:1 + 2 * n_in]
    g_ref, b_ref, *o_refs = refs[1 + 2 * n_in:]
    for r in range(0, h_ref.shape[0], OUTPROJ_SUB_ROWS):
        rows = slice(r, r + OUTPROJ_SUB_ROWS)
        acc = jnp.dot(a_refs[0][rows, :], w_refs[0][...], preferred_element_type=F32)
        for a_ref, w_ref in zip(a_refs[1:], w_refs[1:]):
            acc += jnp.dot(a_ref[rows, :], w_ref[...], preferred_element_type=F32)
        y = DEEPNORM_ALPHA * h_ref[rows, :] + acc
        out = _layer_norm(y, g_ref[...], b_ref[...])
        for o_ref in o_refs:
            o_ref[rows, :] = out.astype(o_ref.dtype)


OUTPROJ_SUB_ROWS = 256


def outproj_ln(h, acts, weights, ln_g, ln_b, *, which=(), tm=512, emit_bf16=False):
    m, d = h.shape
    n_in = len(acts)
    row_spec = pl.BlockSpec((tm, d), lambda i: (i, 0))
    in_specs = [row_spec]
    in_specs += [pl.BlockSpec((tm, a.shape[1]), lambda i: (i, 0)) for a in acts]
    in_specs += [pl.BlockSpec((None,) * len(which) + w.shape[-2:], lambda i: (*which, 0, 0)) for w in weights]
    in_specs += [pl.BlockSpec((1, d), lambda i: (0, 0))] * 2
    out_dtypes = (F32, BF16) if emit_bf16 else (F32,)
    outs = pl.pallas_call(
        functools.partial(_outproj_ln_kernel, n_in),
        grid=(m // tm,),
        in_specs=in_specs,
        out_specs=[row_spec] * len(out_dtypes),
        out_shape=[jax.ShapeDtypeStruct((m, d), dt) for dt in out_dtypes],
        compiler_params=_params(("parallel",)),
        name="outproj_ln",
    )(h, *acts, *weights, ln_g.reshape(1, d), ln_b.reshape(1, d))
    return outs if emit_bf16 else outs[0]


def _swiglu_up_kernel(x_ref, wg_ref, wu_ref, o_ref):
    xb = x_ref[...].astype(BF16)
    gate = jnp.dot(xb, wg_ref[...], preferred_element_type=F32)
    up = jnp.dot(xb, wu_ref[...], preferred_element_type=F32)
    o_ref[...] = (gate * jax.nn.sigmoid(gate) * up).astype(o_ref.dtype)


def swiglu_up(x, w_gate, w_up, which, *, tm=1024, tf=512):
    m, d = x.shape
    f = w_gate.shape[-1]
    w_spec = pl.BlockSpec((None,) * len(which) + (d, tf), lambda j, i: (*which, 0, j))
    return pl.pallas_call(
        _swiglu_up_kernel,
        grid=(f // tf, m // tm),
        in_specs=[pl.BlockSpec((tm, d), lambda j, i: (i, 0)), w_spec, w_spec],
        out_specs=pl.BlockSpec((tm, tf), lambda j, i: (i, j)),
        out_shape=jax.ShapeDtypeStruct((m, f), BF16),
        compiler_params=_params(("parallel", "arbitrary")),
        name="swiglu_up",
    )(x, w_gate, w_up)


FFN_DOWN_ROWS = 256


def _gelu_tanh(x):
    return 0.5 * x * (1.0 + jnp.tanh(math.sqrt(2.0 / math.pi) * (x + 0.044715 * (x * x * x))))


def _softplus(x):
    return jnp.maximum(x, 0.0) + jnp.log1p(jnp.exp(-jnp.abs(x)))


def _rglru_kernel(gate_ref, xr_ref, cw_ref, cb_ref, wa_ref, wx_ref, gab_ref, gxb_ref, lam_ref, o_ref,
                  xext_ref, a_ref, b_ref, hs_ref, carry_ref):
    tt = xr_ref.shape[0]
    halo = 8

    @pl.when(pl.program_id(2) == 0)
    def _():
        xext_ref[0:halo, :] = jnp.zeros((halo, xext_ref.shape[1]), F32)
        carry_ref[...] = jnp.zeros_like(carry_ref)

    xext_ref[halo:halo + tt, :] = xr_ref[...]
    cw = cw_ref[...]
    u = cb_ref[...]
    for tap in range(CONV_WIDTH):
        off = halo - (CONV_WIDTH - 1) + tap
        u = u + cw[tap:tap + 1, :] * xext_ref[off:off + tt, :]
    xext_ref[0:halo, :] = xext_ref[tt:tt + halo, :]

    ub = u.astype(BF16)
    sw = ub.shape[1]

    def gate_pre(w_ref):
        parts = [jnp.dot(ub[:, ks:ks + RNN_GATE_WINDOW], w_ref[0, n], preferred_element_type=F32)
                 for n, ks in enumerate(RNN_GATE_WINDOW_STARTS)]
        return jnp.concatenate(parts, axis=1)[:, :sw]

    r = jax.nn.sigmoid(gate_pre(wa_ref) + gab_ref[...])
    gi = jax.nn.sigmoid(gate_pre(wx_ref) + gxb_ref[...])
    log_a = -LRU_C * r * _softplus(-lam_ref[...])
    a = jnp.exp(log_a)
    a_ref[...] = a
    b_ref[...] = jnp.sqrt(1.0 - a * a) * (gi * u)

    def step(t, h):
        h = a_ref[pl.ds(t, 1), :] * h + b_ref[pl.ds(t, 1), :]
        hs_ref[pl.ds(t, 1), :] = h
        return h

    h_last = lax.fori_loop(0, tt, step, carry_ref[0:1, :], unroll=8)
    carry_ref[0:1, :] = h_last
    o_ref[...] = (hs_ref[...] * _gelu_tanh(gate_ref[...])).astype(o_ref.dtype)


def _gate_windows():
    spans = []
    for lo in range(0, RNN_SUPER_W, RNN_GATE_TILE):
        hi = min(lo + RNN_GATE_TILE, RNN_SUPER_W)
        first, last = lo // RNN_BLOCK, (hi - 1) // RNN_BLOCK
        spans.append((first * RNN_BLOCK // 128 * 128, -(-(last + 1) * RNN_BLOCK // 128) * 128))
    width = max(e - s for s, e in spans)
    return width, tuple(min(s, RNN_SUPER_W - width) for s, _ in spans)


RNN_GATE_TILE = 256
RNN_GATE_WINDOW, RNN_GATE_WINDOW_STARTS = _gate_windows()


def _windowed_block_diag(w):
    n, bs = RNN_BLOCKS_PER_SUPER, RNN_BLOCK
    w = w.reshape(RNN_SUPER, n, bs, bs)
    eye = jnp.eye(n, dtype=w.dtype)
    dense = jnp.einsum('sgij,gh->sgihj', w, eye).reshape(RNN_SUPER, n * bs, n * bs)
    n_tiles = len(RNN_GATE_WINDOW_STARTS)
    dense = jnp.pad(dense, ((0, 0), (0, 0), (0, n_tiles * RNN_GATE_TILE - RNN_SUPER_W)))
    tiles = [dense[:, ks:ks + RNN_GATE_WINDOW, t * RNN_GATE_TILE:(t + 1) * RNN_GATE_TILE]
             for t, ks in enumerate(RNN_GATE_WINDOW_STARTS)]
    return jnp.stack(tiles, axis=1)


def rglru_core(proj_out, conv_w, conv_b, wa, wx, ga_b, gx_b, lam, batch, seq, *, tt=512):
    m = proj_out.shape[0]
    sw = RNN_SUPER_W
    tt = min(tt, seq)
    steps = seq // tt
    vec = lambda v: v.reshape(1, D_RNN)
    vec_spec = pl.BlockSpec((1, sw), lambda b, s, t: (0, s))
    w_spec = pl.BlockSpec((1,) + wa.shape[1:], lambda b, s, t: (s, 0, 0, 0))
    return pl.pallas_call(
        _rglru_kernel,
        grid=(batch, RNN_SUPER, steps),
        in_specs=[
            pl.BlockSpec((tt, sw), lambda b, s, t: (b * steps + t, s)),
            pl.BlockSpec((tt, sw), lambda b, s, t: (b * steps + t, RNN_SUPER + s)),
            pl.BlockSpec((CONV_WIDTH, sw), lambda b, s, t: (0, s)),
            vec_spec,
            w_spec, w_spec,
            vec_spec, vec_spec, vec_spec,
        ],
        out_specs=pl.BlockSpec((tt, sw), lambda b, s, t: (b * steps + t, s)),
        out_shape=jax.ShapeDtypeStruct((m, D_RNN), BF16),
        scratch_shapes=[
            pltpu.VMEM((tt + 8, sw), F32),
            pltpu.VMEM((tt, sw), F32),
            pltpu.VMEM((tt, sw), F32),
            pltpu.VMEM((tt, sw), F32),
            pltpu.VMEM((8, sw), F32),
        ],
        compiler_params=_params(("parallel", "parallel", "arbitrary")),
        name="rglru_core",
    )(proj_out, proj_out, conv_w, vec(conv_b), wa, wx, vec(ga_b), vec(gx_b), vec(lam))


def attention_mixer(h, hb, w_in, gn_g, gn_b, w_out, ln_g, ln_b, batch, seq):
    wb = w_in.astype(BF16)
    cos_tab, sin_tab = _rotary_tables(seq)
    qk = proj_rotary(hb, wb[:, :2 * D_RET], cos_tab, sin_tab, seq)
    rg = proj(hb, wb[:, 3 * D_RET:4 * D_RET], F32)
    vm = proj(hb, jnp.concatenate([wb[:, 2 * D_RET:3 * D_RET], wb[:, 4 * D_RET:]], axis=1), BF16,
              scaled_cols=(D_RET, D_RET + D_MOBA), scale=MOBA_HEAD_DIM ** -0.5)
    ro = retention_gn_gate(qk, vm, rg, gn_g, gn_b, batch, seq)
    mo = moba(vm, batch, seq)
    wo = w_out.astype(BF16)
    return outproj_ln(h, [ro, mo], [wo[:D_RET], wo[D_RET:]], ln_g, ln_b)


def rglru_mixer(h, hb, w_in, conv_w, conv_b, ga_w, ga_b, gx_w, gx_b, lam, w_out, ln_g, ln_b, batch, seq):
    pr = proj(hb, w_in.astype(BF16), F32, tm=512, tn=D_RNN, weights_outer=True)
    y = rglru_core(pr, conv_w, conv_b, _windowed_block_diag(ga_w).astype(BF16),
                   _windowed_block_diag(gx_w).astype(BF16), ga_b, gx_b, lam, batch, seq)
    return outproj_ln(h, [y], [w_out.astype(BF16)], ln_g, ln_b)


def kernel(x, ln_g, ln_b, ffn_w_gate, ffn_w_up, ffn_w_down, attn_w_in, ret_gn_g, ret_gn_b, attn_w_out, rnn_w_in, rnn_conv_w, rnn_conv_b, rnn_gate_a_w, rnn_gate_a_b, rnn_gate_x_w, rnn_gate_x_b, rnn_lambda, rnn_w_out):
    batch, seq, d = x.shape
    h = x.reshape(batch * seq, d)
    w_gate = ffn_w_gate.astype(BF16)
    w_up = ffn_w_up.astype(BF16)
    w_down = (0.5 * ffn_w_down).astype(BF16)
    for layer in range(DEPTH):
        def ffn(hh, half, ln_i, emit_bf16):
            act = swiglu_up(hh, w_gate, w_up, (layer, half))
            return outproj_ln(hh, [act], [w_down], ln_g[layer, ln_i], ln_b[layer, ln_i],
                              which=(layer, half), tm=FFN_DOWN_ROWS, emit_bf16=emit_bf16)
        h, hb = ffn(h, 0, 0, True)
        j = layer // 2
        if layer % 2 == 0:
            h = attention_mixer(h, hb, attn_w_in[j], ret_gn_g[j], ret_gn_b[j], attn_w_out[j],
                                ln_g[layer, 1], ln_b[layer, 1], batch, seq)
        else:
            h = rglru_mixer(h, hb, rnn_w_in[j], rnn_conv_w[j], rnn_conv_b[j], rnn_gate_a_w[j], rnn_gate_a_b[j],
                            rnn_gate_x_w[j], rnn_gate_x_b[j], rnn_lambda[j], rnn_w_out[j],
                            ln_g[layer, 1], ln_b[layer, 1], batch, seq)
        h = ffn(h, 1, 2, False)
    return h.reshape(batch, seq, d)
```
